```python
import math
import jax, jax.numpy as jnp
from jax import lax
import numpy as np

D_MODEL = 4096
BATCH = 2
SEQ = 8192
DEPTH = 1

HG_HEADS = 16
HG_DK = 128
HG_DV = 128
HG_KW = HG_HEADS * HG_DK
HG_VW = HG_HEADS * HG_DV
HG_CHUNK = 64

MLA_HEADS = 16
Q_RANK = 768
KV_RANK = 512
NOPE_D = 128
ROPE_D = 64
QK_D = NOPE_D + ROPE_D
V_D = 128
MLA_VW = MLA_HEADS * V_D
ROPE_THETA = 10000.0
Q_BLOCK = 128

MIX_W = HG_VW + MLA_VW
D_FF = 4 * D_MODEL
EPS = 1e-6

IN_SPLITS = [HG_KW, HG_KW, HG_VW, HG_VW, Q_RANK, KV_RANK, ROPE_D]
IN_W = sum(IN_SPLITS)

kernel_name = "hymba_hgrn2_mla_sqrelu_block"


def rms_norm(x, w):
    xf = x.astype(jnp.float32)
    y = xf * lax.rsqrt(jnp.mean(xf * xf, axis=-1, keepdims=True) + EPS)
    return (y * w.astype(jnp.float32)).astype(x.dtype)


def rope(x, pos):
    d = x.shape[-1]
    inv_freq = ROPE_THETA ** (-jnp.arange(0, d, 2, dtype=jnp.float32) / d)
    ang = pos.astype(jnp.float32)[..., None] * inv_freq
    cos = jnp.cos(ang)[:, :, None, :]
    sin = jnp.sin(ang)[:, :, None, :]
    xf = x.astype(jnp.float32)
    x1, x2 = xf[..., : d // 2], xf[..., d // 2:]
    out = jnp.concatenate([x1 * cos - x2 * sin, x2 * cos + x1 * sin], axis=-1)
    return out.astype(x.dtype)


def hgrn2_mixer(q_raw, f_raw, i_raw, g_raw, lb, out_norm_w):
    B, S, _ = q_raw.shape
    dt = q_raw.dtype
    N = S // HG_CHUNK
    lbf = lb.astype(jnp.float32)
    q = jax.nn.silu(q_raw.astype(jnp.float32))
    f = lbf + (1.0 - lbf) * jax.nn.sigmoid(f_raw.astype(jnp.float32))
    k = 1.0 - f
    logf = jnp.log(f)
    v = i_raw.astype(jnp.float32)

    def to_chunks(t, d):
        return t.reshape(B, N, HG_CHUNK, HG_HEADS, d).transpose(0, 3, 1, 2, 4)

    q, k, logf = (to_chunks(t, HG_DK) for t in (q, k, logf))
    v = to_chunks(v, HG_DV)
    b = jnp.cumsum(logf, axis=3)
    b_last = b[:, :, :, -1:, :]
    q_dec = q * jnp.exp(b)
    k_inv = k * jnp.exp(-b)
    k_tail = k * jnp.exp(b_last - b)
    chunk_decay = jnp.exp(b_last[:, :, :, 0, :])

    a = jnp.einsum('bhncd,bhnsd->bhncs', q_dec, k_inv)
    causal = jnp.tril(jnp.ones((HG_CHUNK, HG_CHUNK), dtype=bool))
    a = jnp.where(causal, a, 0.0)
    o_intra = jnp.einsum('bhncs,bhnsv->bhncv', a, v)

    def step(state, xs):
        qd, kt, vv, dec = xs
        o = jnp.einsum('bhcd,bhdv->bhcv', qd, state)
        state = dec[..., None] * state + jnp.einsum('bhcd,bhcv->bhdv', kt, vv)
        return state, o

    s0 = jnp.zeros((B, HG_HEADS, HG_DK, HG_DV), jnp.float32)
    xs = (jnp.moveaxis(q_dec, 2, 0), jnp.moveaxis(k_tail, 2, 0),
          jnp.moveaxis(v, 2, 0), jnp.moveaxis(chunk_decay, 2, 0))
    _, o_inter = lax.scan(step, s0, xs)
    o = o_intra + jnp.moveaxis(o_inter, 0, 2)
    o = o.transpose(0, 2, 3, 1, 4).reshape(B, S, HG_HEADS, HG_DV)

    o = rms_norm(o, out_norm_w)
    gate = jax.nn.silu(g_raw.astype(jnp.float32)).reshape(B, S, HG_HEADS, HG_DV)
    return (o.astype(jnp.float32) * gate).reshape(B, S, HG_VW).astype(dt)


def mla_mixer(c_q, c_kv, k_rope, positions, q_norm_w, w_uq, kv_norm_w, w_ukv,
              q_head_norm_w, k_head_norm_w):
    B, S, _ = c_q.shape
    dt = c_q.dtype
    q = (rms_norm(c_q, q_norm_w) @ w_uq).reshape(B, S, MLA_HEADS, QK_D)
    kv = (rms_norm(c_kv, kv_norm_w) @ w_ukv).reshape(B, S, MLA_HEADS, NOPE_D + V_D)
    k_nope, v = kv[..., :NOPE_D], kv[..., NOPE_D:]
    k_pe = jnp.broadcast_to(k_rope[:, :, None, :], (B, S, MLA_HEADS, ROPE_D))
    k = jnp.concatenate([k_nope, k_pe], axis=-1)

    q = rms_norm(q, q_head_norm_w)
    k = rms_norm(k, k_head_norm_w)
    q = jnp.concatenate([q[..., :NOPE_D], rope(q[..., NOPE_D:], positions)], axis=-1)
    k = jnp.concatenate([k[..., :NOPE_D], rope(k[..., NOPE_D:], positions)], axis=-1)

    scale = 1.0 / math.sqrt(QK_D)
    kh = k.transpose(0, 2, 1, 3).astype(jnp.float32)
    vh = v.transpose(0, 2, 1, 3).astype(jnp.float32)
    nb = S // Q_BLOCK
    qb = q.transpose(0, 2, 1, 3).reshape(B, MLA_HEADS, nb, Q_BLOCK, QK_D).transpose(2, 0, 1, 3, 4)
    key_idx = jnp.arange(S)

    def attend(args):
        qblk, bi = args
        s = jnp.einsum('bhqd,bhkd->bhqk', qblk.astype(jnp.float32), kh) * scale
        q_idx = bi * Q_BLOCK + jnp.arange(Q_BLOCK)
        mask = key_idx[None, :] <= q_idx[:, None]
        s = jnp.where(mask, s, -jnp.inf)
        p = jax.nn.softmax(s, axis=-1)
        return jnp.einsum('bhqk,bhkv->bhqv', p, vh).astype(dt)

    o = lax.map(attend, (qb, jnp.arange(nb)))
    return o.transpose(1, 0, 3, 2, 4).reshape(B, S, MLA_VW)


def setup_inputs(seed: int = 0) -> dict:
    key = jax.random.key(seed)
    ks = jax.random.split(key, 20)
    f32 = jnp.float32

    def nrm(k, shape, fan_in):
        return jax.random.normal(k, shape, f32) * (fan_in ** -0.5)

    def gain(k, shape):
        return 1.0 + 0.01 * jax.random.normal(k, shape, f32)

    x = jax.random.normal(ks[0], (BATCH, SEQ, D_MODEL), f32)
    positions = (jnp.arange(SEQ, dtype=jnp.int32)[None, :]
                 + jax.random.randint(ks[1], (BATCH, 1), 0, 1024, dtype=jnp.int32))
    return {
        "x": x,
        "positions": positions,
        "norm1_w": gain(ks[2], (DEPTH, D_MODEL)),
        "w_in": nrm(ks[3], (DEPTH, D_MODEL, IN_W), D_MODEL),
        "hgrn_lb": 0.1 * jax.random.normal(ks[4], (DEPTH + 1, HG_KW), f32),
        "hgrn_out_norm_w": gain(ks[5], (DEPTH, HG_DV)),
        "mla_q_norm_w": gain(ks[6], (DEPTH, Q_RANK)),
        "w_uq": nrm(ks[7], (DEPTH, Q_RANK, MLA_HEADS * QK_D), Q_RANK),
        "mla_kv_norm_w": gain(ks[8], (DEPTH, KV_RANK)),
        "w_ukv": nrm(ks[9], (DEPTH, KV_RANK, MLA_HEADS * (NOPE_D + V_D)), KV_RANK),
        "q_head_norm_w": gain(ks[10], (DEPTH, QK_D)),
        "k_head_norm_w": gain(ks[11], (DEPTH, QK_D)),
        "w_o": nrm(ks[12], (DEPTH, MIX_W, D_MODEL), MIX_W),
        "norm2_w": gain(ks[13], (DEPTH, D_MODEL)),
        "w_up": nrm(ks[14], (DEPTH, D_MODEL, D_FF), D_MODEL),
        "w_down": nrm(ks[15], (DEPTH, D_FF, D_MODEL), D_FF),
    }


def reference(x, positions, norm1_w, w_in, hgrn_lb, hgrn_out_norm_w, mla_q_norm_w,
              w_uq, mla_kv_norm_w, w_ukv, q_head_norm_w, k_head_norm_w, w_o,
              norm2_w, w_up, w_down):
    lb_all = jnp.cumsum(jax.nn.softmax(hgrn_lb.astype(jnp.float32), axis=0), axis=0)
    split_pts = list(np.cumsum(IN_SPLITS)[:-1])
    h = x
    for l in range(DEPTH):
        n1 = rms_norm(h, norm1_w[l])
        proj = n1 @ w_in[l]
        q_hg, f_hg, i_hg, g_hg, c_q, c_kv, k_rope = jnp.split(proj, split_pts, axis=-1)
        o_hg = hgrn2_mixer(q_hg, f_hg, i_hg, g_hg, lb_all[l], hgrn_out_norm_w[l])
        o_mla = mla_mixer(c_q, c_kv, k_rope, positions, mla_q_norm_w[l], w_uq[l],
                          mla_kv_norm_w[l], w_ukv[l], q_head_norm_w[l], k_head_norm_w[l])
        mixed = jnp.concatenate([o_hg, o_mla], axis=-1)
        h = h + mixed @ w_o[l]
        n2 = rms_norm(h, norm2_w[l])
        hid = jnp.square(jax.nn.relu(n2 @ w_up[l]))
        h = h + hid @ w_down[l]
    return h
```

```python
import functools
import math

import jax
import jax.numpy as jnp
from jax import lax
from jax.experimental import pallas as pl
from jax.experimental.pallas import tpu as pltpu

F32 = jnp.float32
BF16 = jnp.bfloat16

HG_HEADS = 16
HG_DK = 128
HG_DV = 128
HG_CHUNK = 64
MLA_HEADS = 16
Q_RANK = 768
KV_RANK = 512
NOPE_D = 128
ROPE_D = 64
QK_D = NOPE_D + ROPE_D
V_D = 128
ROPE_THETA = 10000.0
EPS = 1e-6

V7X_LANES = 128
V7X_VMEM_LIMIT_CAP = 56 * 1024 * 1024

QK_PAD = 2 * V7X_LANES
ROPE_HALF = ROPE_D // 2


def _vmem_limit(*byte_counts):
    return int(min(sum(byte_counts), V7X_VMEM_LIMIT_CAP))


def _nbytes(shape, dtype):
    return math.prod(shape) * jnp.dtype(dtype).itemsize


def _rmsnorm_kernel(x_ref, w_ref, o_ref):
    x = x_ref[...]
    ms = jnp.mean(x * x, axis=-1, keepdims=True)
    o_ref[...] = (x * lax.rsqrt(ms + EPS) * w_ref[...]).astype(o_ref.dtype)


def _rmsnorm(x, w, *, tm=256):
    t, d = x.shape
    return pl.pallas_call(
        _rmsnorm_kernel,
        grid=(t // tm,),
        in_specs=[pl.BlockSpec((tm, d), lambda i: (i, 0)),
                  pl.BlockSpec((1, d), lambda i: (0, 0))],
        out_specs=pl.BlockSpec((tm, d), lambda i: (i, 0)),
        out_shape=jax.ShapeDtypeStruct((t, d), BF16),
        compiler_params=pltpu.CompilerParams(
            dimension_semantics=("arbitrary",),
            vmem_limit_bytes=_vmem_limit(4 * _nbytes((tm, d), F32), 2 * _nbytes((tm, d), BF16),
                                         4 * 1024 * 1024)),
        name="rmsnorm",
    )(x, w.reshape(1, d).astype(F32))


def _matmul_kernel(*refs, n_pairs, has_res, act, nk):
    lhs = refs[:n_pairs]
    rhs = refs[n_pairs:2 * n_pairs]
    pos = 2 * n_pairs
    res_ref = refs[pos] if has_res else None
    pos += int(has_res)
    o_ref = refs[pos]
    acc_ref = refs[pos + 1] if nk > 1 else None

    part = jnp.dot(lhs[0][...], rhs[0][...], preferred_element_type=F32)
    for a, b in zip(lhs[1:], rhs[1:]):
        part = part + jnp.dot(a[...], b[...], preferred_element_type=F32)

    def epilogue(v):
        if act == "relu2":
            v = jnp.square(jnp.maximum(v, 0.0))
        if has_res:
            v = v + res_ref[...]
        o_ref[...] = v.astype(o_ref.dtype)

    if nk == 1:
        epilogue(part)
        return

    k = pl.program_id(2)

    @pl.when(k == 0)
    def _():
        acc_ref[...] = part

    @pl.when(jnp.logical_and(k > 0, k < nk - 1))
    def _():
        acc_ref[...] += part

    @pl.when(k == nk - 1)
    def _():
        epilogue(acc_ref[...] + part)


def _matmul(lhs_list, rhs_list, *, residual=None, act=None, out_dtype, tm, tn, tk=None, name):
    m, kdim = lhs_list[0].shape
    n = rhs_list[0].shape[1]
    tk = kdim if tk is None else tk
    nk = kdim // tk
    assert m % tm == 0 and n % tn == 0 and kdim % tk == 0
    n_pairs = len(lhs_list)
    in_specs = ([pl.BlockSpec((tm, tk), lambda i, j, k: (i, k))] * n_pairs
                + [pl.BlockSpec((tk, tn), lambda i, j, k: (k, j))] * n_pairs)
    args = list(lhs_list) + list(rhs_list)
    vmem = [2 * n_pairs * (_nbytes((tm, tk), BF16) + _nbytes((tk, tn), BF16)),
            2 * _nbytes((tm, tn), out_dtype),
            2 * _nbytes((tm, tn), F32)]
    if residual is not None:
        in_specs.append(pl.BlockSpec((tm, tn), lambda i, j, k: (i, j)))
        args.append(residual)
        vmem.append(2 * _nbytes((tm, tn), residual.dtype))
    scratch = []
    if nk > 1:
        scratch.append(pltpu.VMEM((tm, tn), F32))
        vmem.append(_nbytes((tm, tn), F32))
    return pl.pallas_call(
        functools.partial(_matmul_kernel, n_pairs=n_pairs, has_res=residual is not None,
                          act=act, nk=nk),
        grid=(m // tm, n // tn, nk),
        in_specs=in_specs,
        out_specs=pl.BlockSpec((tm, tn), lambda i, j, k: (i, j)),
        out_shape=jax.ShapeDtypeStruct((m, n), out_dtype),
        scratch_shapes=scratch,
        compiler_params=pltpu.CompilerParams(
            dimension_semantics=("arbitrary", "arbitrary", "arbitrary"),
            vmem_limit_bytes=_vmem_limit(*vmem)),
        name=name,
    )(*args)


def _hgrn2_kernel(q_ref, f_ref, i_ref, g_ref, lb_ref, nw_ref, o_ref, state_ref, *, n_chunks):
    c = HG_CHUNK

    @pl.when(pl.program_id(2) == 0)
    def _():
        state_ref[...] = jnp.zeros_like(state_ref)

    lb_raw = lb_ref[...]
    e = jnp.exp(lb_raw - jnp.max(lb_raw, axis=0, keepdims=True))
    lb = e[0:1, :] / jnp.sum(e, axis=0, keepdims=True)
    nw = nw_ref[...]

    row = lax.broadcasted_iota(jnp.int32, (c, c), 0)
    col = lax.broadcasted_iota(jnp.int32, (c, c), 1)
    causal = row >= col
    tri = jnp.where(causal, 1.0, 0.0).astype(BF16)

    def chunk(ci, carry):
        sl = pl.ds(pl.multiple_of(ci * c, c), c)
        q_raw = q_ref[sl, :]
        f_raw = f_ref[sl, :]
        v = i_ref[sl, :]
        g_raw = g_ref[sl, :]

        q = q_raw / (1.0 + jnp.exp(-q_raw))
        f = lb + (1.0 - lb) / (1.0 + jnp.exp(-f_raw))
        k = 1.0 - f
        logf = jnp.log(f)
        hi = logf.astype(BF16)
        lo = (logf - hi.astype(F32)).astype(BF16)
        b = (jnp.dot(tri, hi, preferred_element_type=F32)
             + jnp.dot(tri, lo, preferred_element_type=F32))
        b_last = b[c - 1:c, :]
        q_dec = (q * jnp.exp(b)).astype(BF16)
        k_inv = (k * jnp.exp(-b)).astype(BF16)
        k_tail = (k * jnp.exp(b_last - b)).astype(BF16)
        decay = jnp.exp(b_last)
        vb = v.astype(BF16)

        a = lax.dot_general(q_dec, k_inv, (((1,), (1,)), ((), ())), preferred_element_type=F32)
        a = jnp.where(causal, a, 0.0).astype(BF16)
        o = jnp.dot(a, vb, preferred_element_type=F32)
        state_t = state_ref[...]
        o = o + lax.dot_general(q_dec, state_t.astype(BF16), (((1,), (1,)), ((), ())),
                                preferred_element_type=F32)
        state_ref[...] = state_t * decay + lax.dot_general(
            vb, k_tail, (((0,), (0,)), ((), ())), preferred_element_type=F32)

        ms = jnp.mean(o * o, axis=-1, keepdims=True)
        y = o * lax.rsqrt(ms + EPS) * nw
        gate = g_raw / (1.0 + jnp.exp(-g_raw))
        o_ref[sl, :] = (y * gate).astype(o_ref.dtype)
        return carry

    lax.fori_loop(0, n_chunks, chunk, 0)


def _hgrn2(proj_hg, hgrn_lb, out_norm_w, *, batch, seq, ct=512):
    t = batch * seq
    h = HG_HEADS
    nt = seq // ct
    assert seq % ct == 0 and ct % HG_CHUNK == 0

    def col_spec(group):
        return pl.BlockSpec((ct, HG_DK), lambda b, hh, tt: (b * nt + tt, group * h + hh))

    blk = _nbytes((ct, HG_DK), F32)
    return pl.pallas_call(
        functools.partial(_hgrn2_kernel, n_chunks=ct // HG_CHUNK),
        grid=(batch, h, nt),
        in_specs=[col_spec(0), col_spec(1), col_spec(2), col_spec(3),
                  pl.BlockSpec((hgrn_lb.shape[0], HG_DK), lambda b, hh, tt: (0, hh)),
                  pl.BlockSpec((1, HG_DV), lambda b, hh, tt: (0, 0))],
        out_specs=pl.BlockSpec((ct, HG_DV), lambda b, hh, tt: (b * nt + tt, hh)),
        out_shape=jax.ShapeDtypeStruct((t, h * HG_DV), BF16),
        scratch_shapes=[pltpu.VMEM((HG_DV, HG_DK), F32)],
        compiler_params=pltpu.CompilerParams(
            dimension_semantics=("arbitrary", "arbitrary", "arbitrary"),
            vmem_limit_bytes=_vmem_limit(16 * blk, 8 * 1024 * 1024)),
        name="hgrn2",
    )(proj_hg, proj_hg, proj_hg, proj_hg, hgrn_lb.astype(F32), out_norm_w.reshape(1, HG_DV).astype(F32))


def _mla_prep_kernel(p_ref, pos_ref, qnw_ref, wuq_ref, kvnw_ref, wukv_ref, qhw_ref, khw_ref,
                     invf_ref, sgn_ref, q_out, k_out, v_out, *, scale):
    p = p_ref[...]
    c_q = p[:, :Q_RANK]
    c_kv = p[:, Q_RANK:Q_RANK + KV_RANK]
    k_pe = p[:, Q_RANK + KV_RANK:]

    def rms(x, w):
        return x * lax.rsqrt(jnp.mean(x * x, axis=-1, keepdims=True) + EPS) * w

    q_all = jnp.dot(rms(c_q, qnw_ref[...]).astype(BF16), wuq_ref[...], preferred_element_type=F32)
    kv_all = jnp.dot(rms(c_kv, kvnw_ref[...]).astype(BF16), wukv_ref[...], preferred_element_type=F32)

    ang = pos_ref[...].astype(F32) * invf_ref[...]
    cos = jnp.cos(ang)
    sin = jnp.sin(ang) * sgn_ref[...]

    def rope(x):
        return x * cos + pltpu.roll(x, V7X_LANES // 2, 1) * sin

    qhw = qhw_ref[...]
    khw = khw_ref[...]
    inv_d = 1.0 / QK_D
    k_pe_ss = jnp.sum(k_pe * k_pe, axis=-1, keepdims=True)
    k_pe_rot = rope(k_pe * khw[:, V7X_LANES:])

    for h in range(MLA_HEADS):
        qa = q_all[:, h * QK_PAD:h * QK_PAD + V7X_LANES]
        qb = q_all[:, h * QK_PAD + V7X_LANES:(h + 1) * QK_PAD]
        ss = jnp.sum(qa * qa, axis=-1, keepdims=True) + jnp.sum(qb * qb, axis=-1, keepdims=True)
        r = lax.rsqrt(ss * inv_d + EPS) * scale
        q_out[0, h, :, :V7X_LANES] = (qa * r * qhw[:, :V7X_LANES]).astype(q_out.dtype)
        q_out[0, h, :, V7X_LANES:] = rope(qb * r * qhw[:, V7X_LANES:]).astype(q_out.dtype)

        ka = kv_all[:, h * 2 * V7X_LANES:h * 2 * V7X_LANES + NOPE_D]
        vv = kv_all[:, h * 2 * V7X_LANES + NOPE_D:(h + 1) * 2 * V7X_LANES]
        ssk = jnp.sum(ka * ka, axis=-1, keepdims=True) + k_pe_ss
        rk = lax.rsqrt(ssk * inv_d + EPS)
        k_out[0, h, :, :V7X_LANES] = (ka * rk * khw[:, :V7X_LANES]).astype(k_out.dtype)
        k_out[0, h, :, V7X_LANES:] = (k_pe_rot * rk).astype(k_out.dtype)
        v_out[0, h, :, :] = vv.astype(v_out.dtype)


def _pad_rope_cols(w):
    z = jnp.zeros(w.shape[:-1] + (ROPE_HALF,), w.dtype)
    return jnp.concatenate([w[..., :ROPE_HALF], z, w[..., ROPE_HALF:], z], axis=-1)


def _pad_head_cols(w):
    return jnp.concatenate([w[..., :NOPE_D], _pad_rope_cols(w[..., NOPE_D:])], axis=-1)


def _mla_prep(proj_mla, positions, q_norm_w, w_uq, kv_norm_w, w_ukv, q_head_norm_w, k_head_norm_w,
              *, batch, seq, tm=256):
    t = batch * seq
    nh = MLA_HEADS
    pw = proj_mla.shape[1]
    wuq_p = _pad_head_cols(w_uq.reshape(Q_RANK, nh, QK_D)).reshape(Q_RANK, nh * QK_PAD).astype(BF16)
    wukv = w_ukv.astype(BF16)
    qhw = _pad_head_cols(q_head_norm_w.reshape(1, QK_D)).astype(F32)
    khw = _pad_head_cols(k_head_norm_w.reshape(1, QK_D)).astype(F32)
    inv_freq = ROPE_THETA ** (-jnp.arange(0, ROPE_D, 2, dtype=F32) / ROPE_D)
    invf = _pad_rope_cols(jnp.concatenate([inv_freq, inv_freq]).reshape(1, ROPE_D))
    sgn = _pad_rope_cols(jnp.concatenate([-jnp.ones((ROPE_HALF,), F32),
                                          jnp.ones((ROPE_HALF,), F32)]).reshape(1, ROPE_D))
    nst = seq // tm
    const = lambda i: (0, 0)
    head_out = lambda i: (i // nst, 0, i % nst, 0)
    vmem = [2 * _nbytes((tm, pw), F32), 2 * _nbytes(wuq_p.shape, BF16), 2 * _nbytes(wukv.shape, BF16),
            2 * 2 * _nbytes((nh, tm, QK_PAD), BF16), 2 * _nbytes((nh, tm, V_D), BF16),
            4 * _nbytes((tm, nh * QK_PAD), F32), 4 * 1024 * 1024]
    return pl.pallas_call(
        functools.partial(_mla_prep_kernel, scale=1.0 / math.sqrt(QK_D)),
        grid=(t // tm,),
        in_specs=[pl.BlockSpec((tm, pw), lambda i: (i, 0)),
                  pl.BlockSpec((tm, 1), lambda i: (i, 0)),
                  pl.BlockSpec((1, Q_RANK), const),
                  pl.BlockSpec(wuq_p.shape, const),
                  pl.BlockSpec((1, KV_RANK), const),
                  pl.BlockSpec(wukv.shape, const),
                  pl.BlockSpec((1, QK_PAD), const),
                  pl.BlockSpec((1, QK_PAD), const),
                  pl.BlockSpec((1, V7X_LANES), const),
                  pl.BlockSpec((1, V7X_LANES), const)],
        out_specs=[pl.BlockSpec((1, nh, tm, QK_PAD), head_out),
                   pl.BlockSpec((1, nh, tm, QK_PAD), head_out),
                   pl.BlockSpec((1, nh, tm, V_D), head_out)],
        out_shape=[jax.ShapeDtypeStruct((batch, nh, seq, QK_PAD), BF16),
                   jax.ShapeDtypeStruct((batch, nh, seq, QK_PAD), BF16),
                   jax.ShapeDtypeStruct((batch, nh, seq, V_D), BF16)],
        compiler_params=pltpu.CompilerParams(
            dimension_semantics=("arbitrary",),
            vmem_limit_bytes=_vmem_limit(*vmem)),
        name="mla_prep",
    )(proj_mla, positions.reshape(t, 1), q_norm_w.reshape(1, Q_RANK).astype(F32), wuq_p,
      kv_norm_w.reshape(1, KV_RANK).astype(F32), wukv, qhw, khw, invf, sgn)


def _flash_kernel(q_ref, k_ref, v_ref, o_ref, m_ref, l_ref, acc_ref, *, tq, tk):
    qi = pl.program_id(2)
    ki = pl.program_id(3)
    last_k = ((qi + 1) * tq - 1) // tk

    @pl.when(ki == 0)
    def _():
        m_ref[...] = jnp.full_like(m_ref, -jnp.inf)
        l_ref[...] = jnp.zeros_like(l_ref)
        acc_ref[...] = jnp.zeros_like(acc_ref)

    def update(masked):
        s = lax.dot_general(q_ref[0, 0], k_ref[0, 0], (((1,), (1,)), ((), ())),
                            preferred_element_type=F32)
        if masked:
            q_idx = qi * tq + lax.broadcasted_iota(jnp.int32, (tq, tk), 0)
            k_idx = ki * tk + lax.broadcasted_iota(jnp.int32, (tq, tk), 1)
            s = jnp.where(k_idx <= q_idx, s, -jnp.inf)
        m_prev = m_ref[...]
        m_new = jnp.maximum(m_prev, jnp.max(s, axis=-1, keepdims=True))
        alpha = jnp.exp(m_prev - m_new)
        p = jnp.exp(s - m_new)
        l_ref[...] = alpha * l_ref[...] + jnp.sum(p, axis=-1, keepdims=True)
        acc_ref[...] = alpha * acc_ref[...] + jnp.dot(p.astype(BF16), v_ref[0, 0],
                                                      preferred_element_type=F32)
        m_ref[...] = m_new

    needs_mask = (ki + 1) * tk - 1 > qi * tq

    @pl.when(jnp.logical_and(ki <= last_k, needs_mask))
    def _():
        update(True)

    @pl.when(jnp.logical_and(ki <= last_k, jnp.logical_not(needs_mask)))
    def _():
        update(False)

    @pl.when(ki == last_k)
    def _():
        o_ref[0] = (acc_ref[...] / l_ref[...]).astype(o_ref.dtype)


def _flash_attention(q, k, v, *, tq=512, tk=512):
    b, h, s, dqk = q.shape
    dv = v.shape[-1]

    def kv_map(bb, hh, qi, ki):
        return (bb, hh, jnp.minimum(ki, ((qi + 1) * tq - 1) // tk), 0)

    vmem = [2 * _nbytes((tq, dqk), BF16), 2 * _nbytes((tk, dqk), BF16), 2 * _nbytes((tk, dv), BF16),
            2 * _nbytes((tq, dv), BF16), 3 * _nbytes((tq, V7X_LANES), F32),
            6 * _nbytes((tq, tk), F32), 4 * 1024 * 1024]
    return pl.pallas_call(
        functools.partial(_flash_kernel, tq=tq, tk=tk),
        grid=(b, h, s // tq, s // tk),
        in_specs=[pl.BlockSpec((1, 1, tq, dqk), lambda bb, hh, qi, ki: (bb, hh, qi, 0)),
                  pl.BlockSpec((1, 1, tk, dqk), kv_map),
                  pl.BlockSpec((1, 1, tk, dv), kv_map)],
        out_specs=pl.BlockSpec((1, tq, dv), lambda bb, hh, qi, ki: (bb, qi, hh)),
        out_shape=jax.ShapeDtypeStruct((b, s, h * dv), BF16),
        scratch_shapes=[pltpu.VMEM((tq, 1), F32), pltpu.VMEM((tq, 1), F32), pltpu.VMEM((tq, dv), F32)],
        compiler_params=pltpu.CompilerParams(
            dimension_semantics=("arbitrary", "arbitrary", "arbitrary", "arbitrary"),
            vmem_limit_bytes=_vmem_limit(*vmem)),
        name="mla_flash",
    )(q, k, v)


def kernel(x, positions, norm1_w, w_in, hgrn_lb, hgrn_out_norm_w, mla_q_norm_w, w_uq, mla_kv_norm_w,
           w_ukv, q_head_norm_w, k_head_norm_w, w_o, norm2_w, w_up, w_down):
    batch, seq, d = x.shape
    t = batch * seq
    depth = norm1_w.shape[0]
    hg_w = 2 * HG_HEADS * HG_DK + 2 * HG_HEADS * HG_DV
    h = x.reshape(t, d)
    for l in range(depth):
        n1 = _rmsnorm(h, norm1_w[l])
        w_in_l = w_in[l]
        w_hg = w_in_l[:, :hg_w].astype(BF16)
        w_mla = jnp.concatenate([w_in_l[:, hg_w:hg_w + Q_RANK + KV_RANK],
                                 _pad_rope_cols(w_in_l[:, hg_w + Q_RANK + KV_RANK:])], axis=-1).astype(BF16)
        proj_hg = _matmul([n1], [w_hg], out_dtype=F32, tm=1024, tn=1024, name="in_proj_hg")
        proj_mla = _matmul([n1], [w_mla], out_dtype=F32, tm=512, tn=w_mla.shape[1], name="in_proj_mla")

        assert depth == 1
        o_hg = _hgrn2(proj_hg, hgrn_lb, hgrn_out_norm_w[l], batch=batch, seq=seq)

        q, k, v = _mla_prep(proj_mla, positions, mla_q_norm_w[l], w_uq[l], mla_kv_norm_w[l], w_ukv[l],
                            q_head_norm_w[l], k_head_norm_w[l], batch=batch, seq=seq)
        o_mla = _flash_attention(q, k, v).reshape(t, MLA_HEADS * V_D)

        w_o_l = w_o[l].astype(BF16)
        hg_vw = HG_HEADS * HG_DV
        h = _matmul([o_hg, o_mla], [w_o_l[:hg_vw], w_o_l[hg_vw:]], residual=h, out_dtype=F32,
                    tm=1024, tn=1024, name="out_proj")
        n2 = _rmsnorm(h, norm2_w[l])
        hid = _matmul([n2], [w_up[l].astype(BF16)], act="relu2", out_dtype=BF16, tm=1024, tn=1024,
                      name="mlp_up")
        h = _matmul([hid], [w_down[l].astype(BF16)], residual=h, out_dtype=F32, tm=1024, tn=1024,
                    tk=2048, name="mlp_down")
    return h.reshape(batch, seq, d)
```

```python
import functools
import math

import jax
import jax.numpy as jnp
from jax import lax
from jax.experimental import pallas as pl
from jax.experimental.pallas import tpu as pltpu

F32 = jnp.float32
BF16 = jnp.bfloat16

HG_HEADS = 16
HG_DK = 128
HG_DV = 128
HG_CHUNK = 64
MLA_HEADS = 16
Q_RANK = 768
KV_RANK = 512
NOPE_D = 128
ROPE_D = 64
QK_D = NOPE_D + ROPE_D
V_D = 128
ROPE_THETA = 10000.0
EPS = 1e-6

V7X_LANES = 128
V7X_VMEM_LIMIT_CAP = 56 * 1024 * 1024

QK_PAD = 2 * V7X_LANES
ROPE_HALF = ROPE_D // 2


def _vmem_limit(*byte_counts):
    return int(min(sum(byte_counts), V7X_VMEM_LIMIT_CAP))


def _nbytes(shape, dtype):
    return math.prod(shape) * jnp.dtype(dtype).itemsize


def _rmsnorm_kernel(x_ref, w_ref, o_ref):
    x = x_ref[...]
    ms = jnp.mean(x * x, axis=-1, keepdims=True)
    o_ref[...] = (x * lax.rsqrt(ms + EPS) * w_ref[...]).astype(o_ref.dtype)


def _rmsnorm(x, w, *, tm=256):
    t, d = x.shape
    return pl.pallas_call(
        _rmsnorm_kernel,
        grid=(t // tm,),
        in_specs=[pl.BlockSpec((tm, d), lambda i: (i, 0)),
                  pl.BlockSpec((1, d), lambda i: (0, 0))],
        out_specs=pl.BlockSpec((tm, d), lambda i: (i, 0)),
        out_shape=jax.ShapeDtypeStruct((t, d), BF16),
        compiler_params=pltpu.CompilerParams(
            dimension_semantics=("arbitrary",),
            vmem_limit_bytes=_vmem_limit(4 * _nbytes((tm, d), F32), 2 * _nbytes((tm, d), BF16),
                                         4 * 1024 * 1024)),
        name="rmsnorm",
    )(x, w.reshape(1, d).astype(F32))


def _matmul_kernel(*refs, n_pairs, has_res, act, nk):
    lhs = refs[:n_pairs]
    rhs = refs[n_pairs:2 * n_pairs]
    pos = 2 * n_pairs
    res_ref = refs[pos] if has_res else None
    pos += int(has_res)
    o_ref = refs[pos]
    acc_ref = refs[pos + 1] if nk > 1 else None

    part = jnp.dot(lhs[0][...], rhs[0][...], preferred_element_type=F32)
    for a, b in zip(lhs[1:], rhs[1:]):
        part = part + jnp.dot(a[...], b[...], preferred_element_type=F32)

    def epilogue(v):
        if act == "relu2":
            v = jnp.square(jnp.maximum(v, 0.0))
        if has_res:
            v = v + res_ref[...]
        o_ref[...] = v.astype(o_ref.dtype)

    if nk == 1:
        epilogue(part)
        return

    k = pl.program_id(2)

    @pl.when(k == 0)
    def _():
        acc_ref[...] = part

    @pl.when(jnp.logical_and(k > 0, k < nk - 1))
    def _():
        acc_ref[...] += part

    @pl.when(k == nk - 1)
    def _():
        epilogue(acc_ref[...] + part)


def _matmul(lhs_list, rhs_list, *, residual=None, act=None, out_dtype, tm, tn, tk=None, name):
    m, kdim = lhs_list[0].shape
    n = rhs_list[0].shape[1]
    tk = kdim if tk is None else tk
    nk = kdim // tk
    assert m % tm == 0 and n % tn == 0 and kdim % tk == 0
    n_pairs = len(lhs_list)
    in_specs = ([pl.BlockSpec((tm, tk), lambda i, j, k: (i, k))] * n_pairs
                + [pl.BlockSpec((tk, tn), lambda i, j, k: (k, j))] * n_pairs)
    args = list(lhs_list) + list(rhs_list)
    vmem = [2 * n_pairs * (_nbytes((tm, tk), BF16) + _nbytes((tk, tn), BF16)),
            2 * _nbytes((tm, tn), out_dtype),
            2 * _nbytes((tm, tn), F32)]
    if residual is not None:
        in_specs.append(pl.BlockSpec((tm, tn), lambda i, j, k: (i, j)))
        args.append(residual)
        vmem.append(2 * _nbytes((tm, tn), residual.dtype))
    scratch = []
    if nk > 1:
        scratch.append(pltpu.VMEM((tm, tn), F32))
        vmem.append(_nbytes((tm, tn), F32))
    return pl.pallas_call(
        functools.partial(_matmul_kernel, n_pairs=n_pairs, has_res=residual is not None,
                          act=act, nk=nk),
        grid=(m // tm, n // tn, nk),
        in_specs=in_specs,
        out_specs=pl.BlockSpec((tm, tn), lambda i, j, k: (i, j)),
        out_shape=jax.ShapeDtypeStruct((m, n), out_dtype),
        scratch_shapes=scratch,
        compiler_params=pltpu.CompilerParams(
            dimension_semantics=("arbitrary", "arbitrary", "arbitrary"),
            vmem_limit_bytes=_vmem_limit(*vmem)),
        name=name,
    )(*args)


def _hgrn2_kernel(q_ref, f_ref, i_ref, g_ref, lb_ref, nw_ref, o_ref, state_ref, *, n_chunks, n_heads):
    c = HG_CHUNK

    @pl.when(pl.program_id(2) == 0)
    def _():
        state_ref[...] = jnp.zeros_like(state_ref)

    lb_raw = lb_ref[...]
    e = jnp.exp(lb_raw - jnp.max(lb_raw, axis=0, keepdims=True))
    lb_all = e[0:1, :] / jnp.sum(e, axis=0, keepdims=True)
    nw = nw_ref[...]

    row = lax.broadcasted_iota(jnp.int32, (c, c), 0)
    col = lax.broadcasted_iota(jnp.int32, (c, c), 1)
    causal = row >= col
    tri = jnp.where(causal, 1.0, 0.0).astype(BF16)

    def head_chunk(sl, hd):
        cols = slice(hd * HG_DK, (hd + 1) * HG_DK)
        lb = lb_all[:, cols]
        q_raw = q_ref[sl, cols]
        f_raw = f_ref[sl, cols]
        v = i_ref[sl, cols]
        g_raw = g_ref[sl, cols]

        q = q_raw / (1.0 + jnp.exp(-q_raw))
        f = lb + (1.0 - lb) / (1.0 + jnp.exp(-f_raw))
        k = 1.0 - f
        logf = jnp.log(f)
        hi = logf.astype(BF16)
        lo = (logf - hi.astype(F32)).astype(BF16)
        b = (jnp.dot(tri, hi, preferred_element_type=F32)
             + jnp.dot(tri, lo, preferred_element_type=F32))
        b_last = b[c - 1:c, :]
        q_dec = (q * jnp.exp(b)).astype(BF16)
        k_inv = (k * jnp.exp(-b)).astype(BF16)
        k_tail = (k * jnp.exp(b_last - b)).astype(BF16)
        decay = jnp.exp(b_last)
        vb = v.astype(BF16)

        a = lax.dot_general(q_dec, k_inv, (((1,), (1,)), ((), ())), preferred_element_type=F32)
        a = jnp.where(causal, a, 0.0).astype(BF16)
        o = jnp.dot(a, vb, preferred_element_type=F32)
        state_t = state_ref[hd]
        o = o + lax.dot_general(q_dec, state_t.astype(BF16), (((1,), (1,)), ((), ())),
                                preferred_element_type=F32)
        state_ref[hd] = state_t * decay + lax.dot_general(
            vb, k_tail, (((0,), (0,)), ((), ())), preferred_element_type=F32)

        ms = jnp.mean(o * o, axis=-1, keepdims=True)
        y = o * lax.rsqrt(ms + EPS) * nw
        gate = g_raw / (1.0 + jnp.exp(-g_raw))
        o_ref[sl, cols] = (y * gate).astype(o_ref.dtype)

    def chunk(ci, carry):
        sl = pl.ds(pl.multiple_of(ci * c, c), c)
        for hd in range(n_heads):
            head_chunk(sl, hd)
        return carry

    lax.fori_loop(0, n_chunks, chunk, 0)


def _hgrn2(proj_hg, hgrn_lb, out_norm_w, *, batch, seq, ct, heads_per_step):
    t = batch * seq
    hg = heads_per_step
    n_groups = HG_HEADS // hg
    nt = seq // ct
    assert seq % ct == 0 and ct % HG_CHUNK == 0 and HG_HEADS % hg == 0

    def col_spec(section):
        return pl.BlockSpec((ct, hg * HG_DK), lambda b, gg, tt: (b * nt + tt, section * n_groups + gg))

    blk = _nbytes((ct, hg * HG_DK), F32)
    return pl.pallas_call(
        functools.partial(_hgrn2_kernel, n_chunks=ct // HG_CHUNK, n_heads=hg),
        grid=(batch, n_groups, nt),
        in_specs=[col_spec(0), col_spec(1), col_spec(2), col_spec(3),
                  pl.BlockSpec((hgrn_lb.shape[0], hg * HG_DK), lambda b, gg, tt: (0, gg)),
                  pl.BlockSpec((1, HG_DV), lambda b, gg, tt: (0, 0))],
        out_specs=pl.BlockSpec((ct, hg * HG_DV), lambda b, gg, tt: (b * nt + tt, gg)),
        out_shape=jax.ShapeDtypeStruct((t, HG_HEADS * HG_DV), BF16),
        scratch_shapes=[pltpu.VMEM((hg, HG_DV, HG_DK), F32)],
        compiler_params=pltpu.CompilerParams(
            dimension_semantics=("arbitrary", "arbitrary", "arbitrary"),
            vmem_limit_bytes=_vmem_limit(8 * blk, 2 * _nbytes((ct, hg * HG_DV), BF16),
                                         8 * 1024 * 1024)),
        name="hgrn2",
    )(proj_hg, proj_hg, proj_hg, proj_hg, hgrn_lb.astype(F32), out_norm_w.reshape(1, HG_DV).astype(F32))


def _mla_prep_kernel(p_ref, pos_ref, qnw_ref, wuq_ref, kvnw_ref, wukv_ref, qhw_ref, khw_ref,
                     invf_ref, sgn_ref, qt_out, k_out, vt_out, *, scale):
    p = p_ref[...]
    c_q = p[:, :Q_RANK]
    c_kv = p[:, Q_RANK:Q_RANK + KV_RANK]
    k_pe = p[:, Q_RANK + KV_RANK:]

    def rms(x, w):
        return x * lax.rsqrt(jnp.mean(x * x, axis=-1, keepdims=True) + EPS) * w

    q_all = jnp.dot(rms(c_q, qnw_ref[...]).astype(BF16), wuq_ref[...], preferred_element_type=F32)
    kv_all = jnp.dot(rms(c_kv, kvnw_ref[...]).astype(BF16), wukv_ref[...], preferred_element_type=F32)

    ang = pos_ref[...].astype(F32) * invf_ref[...]
    cos = jnp.cos(ang)
    sin = jnp.sin(ang) * sgn_ref[...]

    def rope(x):
        return x * cos + pltpu.roll(x, V7X_LANES // 2, 1) * sin

    qhw = qhw_ref[...]
    khw = khw_ref[...]
    inv_d = 1.0 / QK_D
    k_pe_ss = jnp.sum(k_pe * k_pe, axis=-1, keepdims=True)
    k_pe_rot = rope(k_pe * khw[:, V7X_LANES:])

    for h in range(MLA_HEADS):
        qa = q_all[:, h * QK_PAD:h * QK_PAD + V7X_LANES]
        qb = q_all[:, h * QK_PAD + V7X_LANES:(h + 1) * QK_PAD]
        ss = jnp.sum(qa * qa, axis=-1, keepdims=True) + jnp.sum(qb * qb, axis=-1, keepdims=True)
        r = lax.rsqrt(ss * inv_d + EPS) * scale
        qt_out[0, h, :V7X_LANES, :] = (qa * r * qhw[:, :V7X_LANES]).T.astype(qt_out.dtype)
        qt_out[0, h, V7X_LANES:, :] = rope(qb * r * qhw[:, V7X_LANES:]).T.astype(qt_out.dtype)

        ka = kv_all[:, h * 2 * V7X_LANES:h * 2 * V7X_LANES + NOPE_D]
        vv = kv_all[:, h * 2 * V7X_LANES + NOPE_D:(h + 1) * 2 * V7X_LANES]
        ssk = jnp.sum(ka * ka, axis=-1, keepdims=True) + k_pe_ss
        rk = lax.rsqrt(ssk * inv_d + EPS)
        k_out[0, h, :, :V7X_LANES] = (ka * rk * khw[:, :V7X_LANES]).astype(k_out.dtype)
        k_out[0, h, :, V7X_LANES:] = (k_pe_rot * rk).astype(k_out.dtype)
        vt_out[0, h, 0] = vv.T.astype(vt_out.dtype)


def _pad_rope_cols(w):
    z = jnp.zeros(w.shape[:-1] + (ROPE_HALF,), w.dtype)
    return jnp.concatenate([w[..., :ROPE_HALF], z, w[..., ROPE_HALF:], z], axis=-1)


def _pad_head_cols(w):
    return jnp.concatenate([w[..., :NOPE_D], _pad_rope_cols(w[..., NOPE_D:])], axis=-1)


def _mla_prep(proj_mla, positions, q_norm_w, w_uq, kv_norm_w, w_ukv, q_head_norm_w, k_head_norm_w,
              *, batch, seq, tm):
    t = batch * seq
    nh = MLA_HEADS
    pw = proj_mla.shape[1]
    wuq_p = _pad_head_cols(w_uq.reshape(Q_RANK, nh, QK_D)).reshape(Q_RANK, nh * QK_PAD).astype(BF16)
    wukv = w_ukv.astype(BF16)
    qhw = _pad_head_cols(q_head_norm_w.reshape(1, QK_D)).astype(F32)
    khw = _pad_head_cols(k_head_norm_w.reshape(1, QK_D)).astype(F32)
    inv_freq = ROPE_THETA ** (-jnp.arange(0, ROPE_D, 2, dtype=F32) / ROPE_D)
    invf = _pad_rope_cols(jnp.concatenate([inv_freq, inv_freq]).reshape(1, ROPE_D))
    sgn = _pad_rope_cols(jnp.concatenate([-jnp.ones((ROPE_HALF,), F32),
                                          jnp.ones((ROPE_HALF,), F32)]).reshape(1, ROPE_D))
    nst = seq // tm
    const = lambda i: (0, 0)
    vmem = [2 * _nbytes((tm, pw), F32), 2 * _nbytes(wuq_p.shape, BF16), 2 * _nbytes(wukv.shape, BF16),
            2 * 2 * _nbytes((nh, tm, QK_PAD), BF16), 2 * _nbytes((nh, tm, V_D), BF16),
            4 * _nbytes((tm, nh * QK_PAD), F32), 4 * 1024 * 1024]
    scale = math.log2(math.e) / math.sqrt(QK_D)
    return pl.pallas_call(
        functools.partial(_mla_prep_kernel, scale=scale),
        grid=(t // tm,),
        in_specs=[pl.BlockSpec((tm, pw), lambda i: (i, 0)),
                  pl.BlockSpec((tm, 1), lambda i: (i, 0)),
                  pl.BlockSpec((1, Q_RANK), const),
                  pl.BlockSpec(wuq_p.shape, const),
                  pl.BlockSpec((1, KV_RANK), const),
                  pl.BlockSpec(wukv.shape, const),
                  pl.BlockSpec((1, QK_PAD), const),
                  pl.BlockSpec((1, QK_PAD), const),
                  pl.BlockSpec((1, V7X_LANES), const),
                  pl.BlockSpec((1, V7X_LANES), const)],
        out_specs=[pl.BlockSpec((1, nh, QK_PAD, tm), lambda i: (i // nst, 0, 0, i % nst)),
                   pl.BlockSpec((1, nh, tm, QK_PAD), lambda i: (i // nst, 0, i % nst, 0)),
                   pl.BlockSpec((1, nh, 1, V_D, tm), lambda i: (i // nst, 0, i % nst, 0, 0))],
        out_shape=[jax.ShapeDtypeStruct((batch, nh, QK_PAD, seq), BF16),
                   jax.ShapeDtypeStruct((batch, nh, seq, QK_PAD), BF16),
                   jax.ShapeDtypeStruct((batch, nh, nst, V_D, tm), BF16)],
        compiler_params=pltpu.CompilerParams(
            dimension_semantics=("arbitrary",),
            vmem_limit_bytes=_vmem_limit(*vmem)),
        name="mla_prep",
    )(proj_mla, positions.reshape(t, 1), q_norm_w.reshape(1, Q_RANK).astype(F32), wuq_p,
      kv_norm_w.reshape(1, KV_RANK).astype(F32), wukv, qhw, khw, invf, sgn)


def _flash_kernel(qt_ref, k_ref, vt_ref, o_ref, acc_ref, st0_ref, st1_ref, *, tq, tk):
    qi = pl.program_id(2)
    qt = qt_ref[0, 0]
    n_full = 2 * qi

    def scores(j, st_ref):
        kj = k_ref[0, 0, pl.ds(pl.multiple_of(j * tk, tk), tk), :]
        st_ref[...] = jnp.dot(kj, qt, preferred_element_type=F32)

    def process(j, st_ref, carry, masked):
        m_prev, l_prev = carry
        st = st_ref[...]
        if masked:
            k_idx = j * tk + lax.broadcasted_iota(jnp.int32, (tk, tq), 0)
            q_idx = qi * tq + lax.broadcasted_iota(jnp.int32, (tk, tq), 1)
            st = jnp.where(k_idx <= q_idx, st, -jnp.inf)
        m_new = jnp.maximum(m_prev, jnp.max(st, axis=0, keepdims=True))
        alpha = jnp.exp2(m_prev - m_new)
        p = jnp.exp2(st - m_new)
        l_new = alpha * l_prev + jnp.sum(p, axis=0, keepdims=True)
        acc_ref[...] = alpha * acc_ref[...] + jnp.dot(vt_ref[0, 0, j], p.astype(BF16),
                                                      preferred_element_type=F32)
        return m_new, l_new

    acc_ref[...] = jnp.zeros_like(acc_ref)
    carry = (jnp.full((1, tq), -jnp.inf, F32), jnp.zeros((1, tq), F32))
    scores(0, st0_ref)

    def pair(i, c):
        a = 2 * i
        scores(a + 1, st1_ref)
        c = process(a, st0_ref, c, False)
        scores(a + 2, st0_ref)
        return process(a + 1, st1_ref, c, False)

    carry = lax.fori_loop(0, qi, pair, carry)
    scores(n_full + 1, st1_ref)
    carry = process(n_full, st0_ref, carry, True)
    carry = process(n_full + 1, st1_ref, carry, True)
    _, l_fin = carry
    o_ref[0] = (acc_ref[...] / l_fin).T.astype(o_ref.dtype)


def _flash_attention(qt, k, vt, *, tq):
    b, h, dqk, s = qt.shape
    nkb, dv, tk = vt.shape[2:]
    assert tq == 2 * tk and s % tq == 0
    vmem = [2 * _nbytes((dqk, tq), BF16), 2 * _nbytes((s, dqk), BF16), 2 * _nbytes((nkb, dv, tk), BF16),
            2 * _nbytes((tq, dv), BF16), 3 * _nbytes((dv, tq), F32),
            6 * _nbytes((tk, tq), F32), 4 * 1024 * 1024]
    return pl.pallas_call(
        functools.partial(_flash_kernel, tq=tq, tk=tk),
        grid=(b, h, s // tq),
        in_specs=[pl.BlockSpec((1, 1, dqk, tq), lambda bb, hh, qi: (bb, hh, 0, qi)),
                  pl.BlockSpec((1, 1, s, dqk), lambda bb, hh, qi: (bb, hh, 0, 0)),
                  pl.BlockSpec((1, 1, nkb, dv, tk), lambda bb, hh, qi: (bb, hh, 0, 0, 0))],
        out_specs=pl.BlockSpec((1, tq, dv), lambda bb, hh, qi: (bb, qi, hh)),
        out_shape=jax.ShapeDtypeStruct((b, s, h * dv), BF16),
        scratch_shapes=[pltpu.VMEM((dv, tq), F32), pltpu.VMEM((tk, tq), F32), pltpu.VMEM((tk, tq), F32)],
        compiler_params=pltpu.CompilerParams(
            dimension_semantics=("arbitrary", "arbitrary", "arbitrary"),
            vmem_limit_bytes=_vmem_limit(*vmem)),
        name="mla_flash",
    )(qt, k, vt)


def kernel(x, positions, norm1_w, w_in, hgrn_lb, hgrn_out_norm_w, mla_q_norm_w, w_uq, mla_kv_norm_w,
           w_ukv, q_head_norm_w, k_head_norm_w, w_o, norm2_w, w_up, w_down):
    batch, seq, d = x.shape
    t = batch * seq
    depth = norm1_w.shape[0]
    hg_w = 2 * HG_HEADS * HG_DK + 2 * HG_HEADS * HG_DV
    h = x.reshape(t, d)
    for l in range(depth):
        n1 = _rmsnorm(h, norm1_w[l])
        w_in_l = w_in[l]
        w_hg = w_in_l[:, :hg_w].astype(BF16)
        w_mla = jnp.concatenate([w_in_l[:, hg_w:hg_w + Q_RANK + KV_RANK],
                                 _pad_rope_cols(w_in_l[:, hg_w + Q_RANK + KV_RANK:])], axis=-1).astype(BF16)
        proj_hg = _matmul([n1], [w_hg], out_dtype=F32, tm=1024, tn=1024, name="in_proj_hg")
        proj_mla = _matmul([n1], [w_mla], out_dtype=F32, tm=512, tn=w_mla.shape[1], name="in_proj_mla")

        assert depth == 1
        o_hg = _hgrn2(proj_hg, hgrn_lb, hgrn_out_norm_w[l], batch=batch, seq=seq, ct=512,
                      heads_per_step=8)

        qt, k, vt = _mla_prep(proj_mla, positions, mla_q_norm_w[l], w_uq[l], mla_kv_norm_w[l], w_ukv[l],
                              q_head_norm_w[l], k_head_norm_w[l], batch=batch, seq=seq, tm=256)
        o_mla = _flash_attention(qt, k, vt, tq=512).reshape(t, MLA_HEADS * V_D)

        w_o_l = w_o[l].astype(BF16)
        hg_vw = HG_HEADS * HG_DV
        h = _matmul([o_hg, o_mla], [w_o_l[:hg_vw], w_o_l[hg_vw:]], residual=h, out_dtype=F32,
                    tm=1024, tn=1024, name="out_proj")
        n2 = _rmsnorm(h, norm2_w[l])
        hid = _matmul([n2], [w_up[l].astype(BF16)], act="relu2", out_dtype=BF16, tm=1024, tn=1024,
                      name="mlp_up")
        h = _matmul([hid], [w_down[l].astype(BF16)], residual=h, out_dtype=F32, tm=1024, tn=1024,
                    tk=2048, name="mlp_down")
    return h.reshape(batch, seq, d)
```

```python
import functools
import math

import jax
import jax.numpy as jnp
from jax import lax
from jax.experimental import pallas as pl
from jax.experimental.pallas import tpu as pltpu

F32 = jnp.float32
BF16 = jnp.bfloat16

HG_HEADS = 16
HG_DK = 128
HG_DV = 128
HG_CHUNK = 64
MLA_HEADS = 16
Q_RANK = 768
KV_RANK = 512
NOPE_D = 128
ROPE_D = 64
QK_D = NOPE_D + ROPE_D
V_D = 128
ROPE_THETA = 10000.0
EPS = 1e-6

V7X_LANES = 128
V7X_BF16_SUBLANES = 16
V7X_VMEM_LIMIT_CAP = 56 * 1024 * 1024

VT_ROWS = V_D + V7X_BF16_SUBLANES

QK_PAD = 2 * V7X_LANES
ROPE_HALF = ROPE_D // 2


def _vmem_limit(*byte_counts):
    return int(min(sum(byte_counts), V7X_VMEM_LIMIT_CAP))


def _nbytes(shape, dtype):
    return math.prod(shape) * jnp.dtype(dtype).itemsize


def _rmsnorm_kernel(x_ref, w_ref, o_ref):
    x = x_ref[...]
    ms = jnp.mean(x * x, axis=-1, keepdims=True)
    o_ref[...] = (x * lax.rsqrt(ms + EPS) * w_ref[...]).astype(o_ref.dtype)


def _rmsnorm(x, w, *, tm=256):
    t, d = x.shape
    return pl.pallas_call(
        _rmsnorm_kernel,
        grid=(t // tm,),
        in_specs=[pl.BlockSpec((tm, d), lambda i: (i, 0)),
                  pl.BlockSpec((1, d), lambda i: (0, 0))],
        out_specs=pl.BlockSpec((tm, d), lambda i: (i, 0)),
        out_shape=jax.ShapeDtypeStruct((t, d), BF16),
        compiler_params=pltpu.CompilerParams(
            dimension_semantics=("arbitrary",),
            vmem_limit_bytes=_vmem_limit(4 * _nbytes((tm, d), F32), 2 * _nbytes((tm, d), BF16),
                                         4 * 1024 * 1024)),
        name="rmsnorm",
    )(x, w.reshape(1, d).astype(F32))


def _matmul_kernel(*refs, n_pairs, has_res, act, nk):
    lhs = refs[:n_pairs]
    rhs = refs[n_pairs:2 * n_pairs]
    pos = 2 * n_pairs
    res_ref = refs[pos] if has_res else None
    pos += int(has_res)
    o_ref = refs[pos]
    acc_ref = refs[pos + 1] if nk > 1 else None

    def product():
        part = jnp.dot(lhs[0][...], rhs[0][...], preferred_element_type=F32)
        for a, b in zip(lhs[1:], rhs[1:]):
            part = part + jnp.dot(a[...], b[...], preferred_element_type=F32)
        return part

    def epilogue(v):
        if act == "relu2":
            v = jnp.square(jnp.maximum(v, 0.0))
        if has_res:
            v = v + res_ref[...]
        o_ref[...] = v.astype(o_ref.dtype)

    if nk == 1:
        epilogue(product())
        return

    k = pl.program_id(2)

    @pl.when(k == 0)
    def _():
        acc_ref[...] = product()

    @pl.when(jnp.logical_and(k > 0, k < nk - 1))
    def _():
        acc_ref[...] += product()

    @pl.when(k == nk - 1)
    def _():
        epilogue(acc_ref[...] + product())


def _matmul(lhs_list, rhs_list, *, residual=None, act=None, out_dtype, tm, tn, tk=None, name):
    m, kdim = lhs_list[0].shape
    n = rhs_list[0].shape[1]
    tk = kdim if tk is None else tk
    nk = kdim // tk
    assert m % tm == 0 and n % tn == 0 and kdim % tk == 0
    n_pairs = len(lhs_list)
    in_specs = ([pl.BlockSpec((tm, tk), lambda i, j, k: (i, k))] * n_pairs
                + [pl.BlockSpec((tk, tn), lambda i, j, k: (k, j))] * n_pairs)
    args = list(lhs_list) + list(rhs_list)
    vmem = [2 * n_pairs * (_nbytes((tm, tk), BF16) + _nbytes((tk, tn), BF16)),
            2 * _nbytes((tm, tn), out_dtype),
            2 * _nbytes((tm, tn), F32)]
    if residual is not None:
        in_specs.append(pl.BlockSpec((tm, tn), lambda i, j, k: (i, j)))
        args.append(residual)
        vmem.append(2 * _nbytes((tm, tn), residual.dtype))
    scratch = []
    if nk > 1:
        scratch.append(pltpu.VMEM((tm, tn), F32))
        vmem.append(_nbytes((tm, tn), F32))
    return pl.pallas_call(
        functools.partial(_matmul_kernel, n_pairs=n_pairs, has_res=residual is not None,
                          act=act, nk=nk),
        grid=(m // tm, n // tn, nk),
        in_specs=in_specs,
        out_specs=pl.BlockSpec((tm, tn), lambda i, j, k: (i, j)),
        out_shape=jax.ShapeDtypeStruct((m, n), out_dtype),
        scratch_shapes=scratch,
        compiler_params=pltpu.CompilerParams(
            dimension_semantics=("arbitrary", "arbitrary", "arbitrary"),
            vmem_limit_bytes=_vmem_limit(*vmem)),
        name=name,
    )(*args)


def _hgrn2_kernel(q_ref, f_ref, i_ref, g_ref, lb_ref, nw_ref, o_ref, state_ref, *, n_chunks, n_heads):
    c = HG_CHUNK

    @pl.when(pl.program_id(2) == 0)
    def _():
        state_ref[...] = jnp.zeros_like(state_ref)

    lb_raw = lb_ref[...]
    e = jnp.exp(lb_raw - jnp.max(lb_raw, axis=0, keepdims=True))
    lb_all = e[0:1, :] / jnp.sum(e, axis=0, keepdims=True)
    nw = nw_ref[...]

    row = lax.broadcasted_iota(jnp.int32, (c, c), 0)
    col = lax.broadcasted_iota(jnp.int32, (c, c), 1)
    causal = row >= col
    tri = jnp.where(causal, 1.0, 0.0).astype(BF16)

    nt_dims = (((1,), (1,)), ((), ()))
    tn_dims = (((0,), (0,)), ((), ()))
    head_cols = [slice(hd * HG_DK, (hd + 1) * HG_DK) for hd in range(n_heads)]

    def chunk(ci, carry):
        sl = pl.ds(pl.multiple_of(ci * c, c), c)
        q_raw = q_ref[sl, :]
        f_raw = f_ref[sl, :]
        q = q_raw / (1.0 + jnp.exp(-q_raw))
        f = lb_all + (1.0 - lb_all) / (1.0 + jnp.exp(-f_raw))
        k = 1.0 - f
        logf = jnp.log(f)
        hi = logf.astype(BF16)
        lo = (logf - hi.astype(F32)).astype(BF16)
        b = (jnp.dot(tri, hi, preferred_element_type=F32)
             + jnp.dot(tri, lo, preferred_element_type=F32))
        b_last = b[c - 1:c, :]
        q_dec = (q * jnp.exp(b)).astype(BF16)
        k_inv = (k * jnp.exp(-b)).astype(BF16)
        k_tail = (k * jnp.exp(b_last - b)).astype(BF16)
        decay = jnp.exp(b_last)
        vb = i_ref[sl, :].astype(BF16)

        a = [lax.dot_general(q_dec[:, s], k_inv[:, s], nt_dims, preferred_element_type=F32)
             for s in head_cols]
        a = [jnp.where(causal, x, 0.0).astype(BF16) for x in a]
        states = [state_ref[hd] for hd in range(n_heads)]
        o = [jnp.dot(a[hd], vb[:, s], preferred_element_type=F32)
             + lax.dot_general(q_dec[:, s], states[hd].astype(BF16), nt_dims, preferred_element_type=F32)
             for hd, s in enumerate(head_cols)]
        u = [lax.dot_general(vb[:, s], k_tail[:, s], tn_dims, preferred_element_type=F32)
             for s in head_cols]
        for hd, s in enumerate(head_cols):
            state_ref[hd] = states[hd] * decay[:, s] + u[hd]

        y = [x * lax.rsqrt(jnp.mean(x * x, axis=-1, keepdims=True) + EPS) * nw for x in o]
        g_raw = g_ref[sl, :]
        gate = g_raw / (1.0 + jnp.exp(-g_raw))
        o_ref[sl, :] = (jnp.concatenate(y, axis=1) * gate).astype(o_ref.dtype)
        return carry

    lax.fori_loop(0, n_chunks, chunk, 0, unroll=2)


def _hgrn2(proj_hg, hgrn_lb, out_norm_w, *, batch, seq, ct, heads_per_step):
    t = batch * seq
    hg = heads_per_step
    n_groups = HG_HEADS // hg
    nt = seq // ct
    assert seq % ct == 0 and ct % HG_CHUNK == 0 and HG_HEADS % hg == 0

    def col_spec(section):
        return pl.BlockSpec((ct, hg * HG_DK), lambda b, gg, tt: (b * nt + tt, section * n_groups + gg))

    blk = _nbytes((ct, hg * HG_DK), F32)
    return pl.pallas_call(
        functools.partial(_hgrn2_kernel, n_chunks=ct // HG_CHUNK, n_heads=hg),
        grid=(batch, n_groups, nt),
        in_specs=[col_spec(0), col_spec(1), col_spec(2), col_spec(3),
                  pl.BlockSpec((hgrn_lb.shape[0], hg * HG_DK), lambda b, gg, tt: (0, gg)),
                  pl.BlockSpec((1, HG_DV), lambda b, gg, tt: (0, 0))],
        out_specs=pl.BlockSpec((ct, hg * HG_DV), lambda b, gg, tt: (b * nt + tt, gg)),
        out_shape=jax.ShapeDtypeStruct((t, HG_HEADS * HG_DV), BF16),
        scratch_shapes=[pltpu.VMEM((hg, HG_DV, HG_DK), F32)],
        compiler_params=pltpu.CompilerParams(
            dimension_semantics=("arbitrary", "arbitrary", "arbitrary"),
            vmem_limit_bytes=_vmem_limit(8 * blk, 2 * _nbytes((ct, hg * HG_DV), BF16),
                                         8 * 1024 * 1024)),
        name="hgrn2",
    )(proj_hg, proj_hg, proj_hg, proj_hg, hgrn_lb.astype(F32), out_norm_w.reshape(1, HG_DV).astype(F32))


def _mla_prep_kernel(p_ref, pos_ref, qnw_ref, wuq_ref, kvnw_ref, wukv_ref, qhw_ref, khw_ref,
                     invf_ref, sgn_ref, qt_out, k_out, vt_out, *, scale):
    p = p_ref[...]
    c_q = p[:, :Q_RANK]
    c_kv = p[:, Q_RANK:Q_RANK + KV_RANK]
    k_pe = p[:, Q_RANK + KV_RANK:]

    def rms(x, w):
        return x * lax.rsqrt(jnp.mean(x * x, axis=-1, keepdims=True) + EPS) * w

    q_all = jnp.dot(rms(c_q, qnw_ref[...]).astype(BF16), wuq_ref[...], preferred_element_type=F32)
    kv_all = jnp.dot(rms(c_kv, kvnw_ref[...]).astype(BF16), wukv_ref[...], preferred_element_type=F32)

    ang = pos_ref[...].astype(F32) * invf_ref[...]
    cos = jnp.cos(ang)
    sin = jnp.sin(ang) * sgn_ref[...]

    def rope(x):
        return x * cos + pltpu.roll(x, V7X_LANES // 2, 1) * sin

    qhw = qhw_ref[...]
    khw = khw_ref[...]
    inv_d = 1.0 / QK_D
    k_pe_ss = jnp.sum(k_pe * k_pe, axis=-1, keepdims=True)
    k_pe_rot = rope(k_pe * khw[:, V7X_LANES:])

    for h in range(MLA_HEADS):
        qa = q_all[:, h * QK_PAD:h * QK_PAD + V7X_LANES]
        qb = q_all[:, h * QK_PAD + V7X_LANES:(h + 1) * QK_PAD]
        ss = jnp.sum(qa * qa, axis=-1, keepdims=True) + jnp.sum(qb * qb, axis=-1, keepdims=True)
        r = lax.rsqrt(ss * inv_d + EPS) * scale
        qt_out[0, h, :V7X_LANES, :] = (qa * r * qhw[:, :V7X_LANES]).T.astype(qt_out.dtype)
        qt_out[0, h, V7X_LANES:, :] = rope(qb * r * qhw[:, V7X_LANES:]).T.astype(qt_out.dtype)

        ka = kv_all[:, h * 2 * V7X_LANES:h * 2 * V7X_LANES + NOPE_D]
        vv = kv_all[:, h * 2 * V7X_LANES + NOPE_D:(h + 1) * 2 * V7X_LANES]
        ssk = jnp.sum(ka * ka, axis=-1, keepdims=True) + k_pe_ss
        rk = lax.rsqrt(ssk * inv_d + EPS)
        k_out[0, h, :, :V7X_LANES] = (ka * rk * khw[:, :V7X_LANES]).astype(k_out.dtype)
        k_out[0, h, :, V7X_LANES:] = (k_pe_rot * rk).astype(k_out.dtype)
        vt_out[0, h, 0, :V_D, :] = vv.T.astype(vt_out.dtype)
        vt_out[0, h, 0, V_D:, :] = jnp.ones((VT_ROWS - V_D, vv.shape[0]), vt_out.dtype)


def _pad_rope_cols(w):
    z = jnp.zeros(w.shape[:-1] + (ROPE_HALF,), w.dtype)
    return jnp.concatenate([w[..., :ROPE_HALF], z, w[..., ROPE_HALF:], z], axis=-1)


def _pad_head_cols(w):
    return jnp.concatenate([w[..., :NOPE_D], _pad_rope_cols(w[..., NOPE_D:])], axis=-1)


def _mla_prep(proj_mla, positions, q_norm_w, w_uq, kv_norm_w, w_ukv, q_head_norm_w, k_head_norm_w,
              *, batch, seq, tm):
    t = batch * seq
    nh = MLA_HEADS
    pw = proj_mla.shape[1]
    wuq_p = _pad_head_cols(w_uq.reshape(Q_RANK, nh, QK_D)).reshape(Q_RANK, nh * QK_PAD).astype(BF16)
    wukv = w_ukv.astype(BF16)
    qhw = _pad_head_cols(q_head_norm_w.reshape(1, QK_D)).astype(F32)
    khw = _pad_head_cols(k_head_norm_w.reshape(1, QK_D)).astype(F32)
    inv_freq = ROPE_THETA ** (-jnp.arange(0, ROPE_D, 2, dtype=F32) / ROPE_D)
    invf = _pad_rope_cols(jnp.concatenate([inv_freq, inv_freq]).reshape(1, ROPE_D))
    sgn = _pad_rope_cols(jnp.concatenate([-jnp.ones((ROPE_HALF,), F32),
                                          jnp.ones((ROPE_HALF,), F32)]).reshape(1, ROPE_D))
    nst = seq // tm
    const = lambda i: (0, 0)
    vmem = [2 * _nbytes((tm, pw), F32), 2 * _nbytes(wuq_p.shape, BF16), 2 * _nbytes(wukv.shape, BF16),
            2 * 2 * _nbytes((nh, tm, QK_PAD), BF16), 2 * _nbytes((nh, tm, VT_ROWS), BF16),
            4 * _nbytes((tm, nh * QK_PAD), F32), 4 * 1024 * 1024]
    scale = math.log2(math.e) / math.sqrt(QK_D)
    return pl.pallas_call(
        functools.partial(_mla_prep_kernel, scale=scale),
        grid=(t // tm,),
        in_specs=[pl.BlockSpec((tm, pw), lambda i: (i, 0)),
                  pl.BlockSpec((tm, 1), lambda i: (i, 0)),
                  pl.BlockSpec((1, Q_RANK), const),
                  pl.BlockSpec(wuq_p.shape, const),
                  pl.BlockSpec((1, KV_RANK), const),
                  pl.BlockSpec(wukv.shape, const),
                  pl.BlockSpec((1, QK_PAD), const),
                  pl.BlockSpec((1, QK_PAD), const),
                  pl.BlockSpec((1, V7X_LANES), const),
                  pl.BlockSpec((1, V7X_LANES), const)],
        out_specs=[pl.BlockSpec((1, nh, QK_PAD, tm), lambda i: (i // nst, 0, 0, i % nst)),
                   pl.BlockSpec((1, nh, tm, QK_PAD), lambda i: (i // nst, 0, i % nst, 0)),
                   pl.BlockSpec((1, nh, 1, VT_ROWS, tm), lambda i: (i // nst, 0, i % nst, 0, 0))],
        out_shape=[jax.ShapeDtypeStruct((batch, nh, QK_PAD, seq), BF16),
                   jax.ShapeDtypeStruct((batch, nh, seq, QK_PAD), BF16),
                   jax.ShapeDtypeStruct((batch, nh, nst, VT_ROWS, tm), BF16)],
        compiler_params=pltpu.CompilerParams(
            dimension_semantics=("arbitrary",),
            vmem_limit_bytes=_vmem_limit(*vmem)),
        name="mla_prep",
    )(proj_mla, positions.reshape(t, 1), q_norm_w.reshape(1, Q_RANK).astype(F32), wuq_p,
      kv_norm_w.reshape(1, KV_RANK).astype(F32), wukv, qhw, khw, invf, sgn)


def _flash_kernel(qt_ref, k_ref, vt_ref, o_ref, acc_ref, st_a, st_b, p_a, p_b, al_a, al_b, cm_a, cm_b,
                  *, tq, tk):
    qi = pl.program_id(2)
    n_full = 2 * qi
    vt_tile = vt_ref.shape[-1]
    n_vt = tk // vt_tile
    half = tq // 2
    every = slice(0, tq)
    right = slice(half, tq)

    def scores(blk, st_ref, cm_ref, cols, masked):
        kj = k_ref[0, 0, pl.ds(pl.multiple_of(blk * tk, tk), tk), :]
        st = jnp.dot(kj, qt_ref[0, 0, :, cols], preferred_element_type=F32)
        if masked:
            st = causal_mask(blk, st, cols)
        st_ref[:, cols] = st
        cm_ref[:, cols] = jnp.max(st, axis=0, keepdims=True)

    def causal_mask(blk, st, cols):
        n_cols = cols.stop - cols.start
        k_idx = blk * tk + lax.broadcasted_iota(jnp.int32, (tk, n_cols), 0)
        q_idx = qi * tq + cols.start + lax.broadcasted_iota(jnp.int32, (tk, n_cols), 1)
        return jnp.where(k_idx <= q_idx, st, -jnp.inf)

    def remask(blk, st_ref, cm_ref, cols):
        st = causal_mask(blk, st_ref[:, cols], cols)
        st_ref[:, cols] = st
        cm_ref[:, cols] = jnp.max(st, axis=0, keepdims=True)

    def softmax(st_ref, cm_ref, p_ref, al_ref, m_prev, cols):
        m_new = jnp.maximum(m_prev, cm_ref[:, cols])
        p_ref[:, cols] = jnp.exp2(st_ref[:, cols] - m_new).astype(BF16)
        al_ref[:, cols] = jnp.exp2(m_prev - m_new)
        return m_new

    def accumulate(blk, p_ref, al_ref, cols):
        vt = jnp.concatenate([vt_ref[0, 0, blk * n_vt + t] for t in range(n_vt)], axis=1)
        acc_ref[:, cols] = al_ref[:, cols] * acc_ref[:, cols] + jnp.dot(
            vt, p_ref[:, cols], preferred_element_type=F32)

    acc_ref[...] = jnp.zeros_like(acc_ref)
    p_b[...] = jnp.zeros_like(p_b)
    al_b[...] = jnp.ones_like(al_b)
    m_run = jnp.full((1, tq), -jnp.inf, F32)
    scores(0, st_a, cm_a, every, False)

    def body(i, m):
        h = 2 * i
        scores(h + 1, st_b, cm_b, every, False)
        m = softmax(st_a, cm_a, p_a, al_a, m, every)
        accumulate(jnp.maximum(h - 1, 0), p_b, al_b, every)
        scores(h + 2, st_a, cm_a, every, False)
        m = softmax(st_b, cm_b, p_b, al_b, m, every)
        accumulate(h, p_a, al_a, every)
        return m

    m_run = lax.fori_loop(0, qi, body, m_run)
    scores(n_full + 1, st_b, cm_b, right, True)
    remask(n_full, st_a, cm_a, every)
    accumulate(jnp.maximum(n_full - 1, 0), p_b, al_b, every)
    m_run = softmax(st_a, cm_a, p_a, al_a, m_run, every)
    accumulate(n_full, p_a, al_a, every)
    softmax(st_b, cm_b, p_b, al_b, m_run[:, right], right)
    accumulate(n_full + 1, p_b, al_b, right)
    dv = o_ref.shape[-1]
    o_ref[0] = (acc_ref[:dv, :] / acc_ref[dv:dv + 1, :]).T.astype(o_ref.dtype)


def _flash_attention(qt, k, vt, *, tq):
    b, h, dqk, s = qt.shape
    nvb, vt_rows, tv = vt.shape[2:]
    dv = V_D
    tk = tq // 2
    assert tk % tv == 0 and s % tq == 0 and vt_rows == VT_ROWS
    vmem = [2 * _nbytes((dqk, tq), BF16), 2 * _nbytes((s, dqk), BF16), 2 * _nbytes((nvb, vt_rows, tv), BF16),
            2 * _nbytes((tq, dv), BF16), 3 * _nbytes((vt_rows, tq), F32),
            2 * _nbytes((tk, tq), F32), 2 * _nbytes((tk, tq), BF16),
            3 * _nbytes((tk, tq), F32), 4 * 1024 * 1024]
    return pl.pallas_call(
        functools.partial(_flash_kernel, tq=tq, tk=tk),
        grid=(b, h, s // tq),
        in_specs=[pl.BlockSpec((1, 1, dqk, tq), lambda bb, hh, qi: (bb, hh, 0, qi)),
                  pl.BlockSpec((1, 1, s, dqk), lambda bb, hh, qi: (bb, hh, 0, 0)),
                  pl.BlockSpec((1, 1, nvb, vt_rows, tv), lambda bb, hh, qi: (bb, hh, 0, 0, 0))],
        out_specs=pl.BlockSpec((1, tq, dv), lambda bb, hh, qi: (bb, qi, hh)),
        out_shape=jax.ShapeDtypeStruct((b, s, h * dv), BF16),
        scratch_shapes=[pltpu.VMEM((vt_rows, tq), F32),
                        pltpu.VMEM((tk, tq), F32), pltpu.VMEM((tk, tq), F32),
                        pltpu.VMEM((tk, tq), BF16), pltpu.VMEM((tk, tq), BF16),
                        pltpu.VMEM((1, tq), F32), pltpu.VMEM((1, tq), F32),
                        pltpu.VMEM((1, tq), F32), pltpu.VMEM((1, tq), F32)],
        compiler_params=pltpu.CompilerParams(
            dimension_semantics=("arbitrary", "arbitrary", "arbitrary"),
            vmem_limit_bytes=_vmem_limit(*vmem)),
        name="mla_flash",
    )(qt, k, vt)


def kernel(x, positions, norm1_w, w_in, hgrn_lb, hgrn_out_norm_w, mla_q_norm_w, w_uq, mla_kv_norm_w,
           w_ukv, q_head_norm_w, k_head_norm_w, w_o, norm2_w, w_up, w_down):
    batch, seq, d = x.shape
    t = batch * seq
    depth = norm1_w.shape[0]
    hg_w = 2 * HG_HEADS * HG_DK + 2 * HG_HEADS * HG_DV
    h = x.reshape(t, d)
    for l in range(depth):
        n1 = _rmsnorm(h, norm1_w[l])
        w_in_l = w_in[l]
        w_hg = w_in_l[:, :hg_w].astype(BF16)
        w_mla = jnp.concatenate([w_in_l[:, hg_w:hg_w + Q_RANK + KV_RANK],
                                 _pad_rope_cols(w_in_l[:, hg_w + Q_RANK + KV_RANK:])], axis=-1).astype(BF16)
        proj_hg = _matmul([n1], [w_hg], out_dtype=F32, tm=1024, tn=1024, name="in_proj_hg")
        proj_mla = _matmul([n1], [w_mla], out_dtype=F32, tm=512, tn=w_mla.shape[1], name="in_proj_mla")

        assert depth == 1
        o_hg = _hgrn2(proj_hg, hgrn_lb, hgrn_out_norm_w[l], batch=batch, seq=seq, ct=512,
                      heads_per_step=8)

        qt, k, vt = _mla_prep(proj_mla, positions, mla_q_norm_w[l], w_uq[l], mla_kv_norm_w[l], w_ukv[l],
                              q_head_norm_w[l], k_head_norm_w[l], batch=batch, seq=seq, tm=256)
        o_mla = _flash_attention(qt, k, vt, tq=1024).reshape(t, MLA_HEADS * V_D)

        w_o_l = w_o[l].astype(BF16)
        hg_vw = HG_HEADS * HG_DV
        h = _matmul([o_hg, o_mla], [w_o_l[:hg_vw], w_o_l[hg_vw:]], residual=h, out_dtype=F32,
                    tm=1024, tn=1024, name="out_proj")
        n2 = _rmsnorm(h, norm2_w[l])
        hid = _matmul([n2], [w_up[l].astype(BF16)], act="relu2", out_dtype=BF16, tm=1024, tn=1024,
                      name="mlp_up")
        h = _matmul([hid], [w_down[l].astype(BF16)], residual=h, out_dtype=F32, tm=1024, tn=1024,
                    tk=2048, name="mlp_down")
    return h.reshape(batch, seq, d)
```

```python
import functools
import math

import jax
import jax.numpy as jnp
from jax import lax
from jax.experimental import pallas as pl
from jax.experimental.pallas import tpu as pltpu

F32 = jnp.float32
BF16 = jnp.bfloat16

HG_HEADS = 16
HG_DK = 128
HG_DV = 128
HG_CHUNK = 64
MLA_HEADS = 16
Q_RANK = 768
KV_RANK = 512
NOPE_D = 128
ROPE_D = 64
QK_D = NOPE_D + ROPE_D
V_D = 128
ROPE_THETA = 10000.0
EPS = 1e-6

V7X_LANES = 128
V7X_BF16_SUBLANES = 16
V7X_VMEM_LIMIT_CAP = 56 * 1024 * 1024

VT_ROWS = V_D + V7X_BF16_SUBLANES

QK_PAD = 2 * V7X_LANES
ROPE_HALF = ROPE_D // 2


def _vmem_limit(*byte_counts):
    return int(min(sum(byte_counts), V7X_VMEM_LIMIT_CAP))


def _nbytes(shape, dtype):
    return math.prod(shape) * jnp.dtype(dtype).itemsize


def _rmsnorm_kernel(x_ref, w_ref, o_ref):
    x = x_ref[...]
    ms = jnp.mean(x * x, axis=-1, keepdims=True)
    o_ref[...] = (x * lax.rsqrt(ms + EPS) * w_ref[...]).astype(o_ref.dtype)


def _rmsnorm(x, w, *, tm=256):
    t, d = x.shape
    return pl.pallas_call(
        _rmsnorm_kernel,
        grid=(t // tm,),
        in_specs=[pl.BlockSpec((tm, d), lambda i: (i, 0)),
                  pl.BlockSpec((1, d), lambda i: (0, 0))],
        out_specs=pl.BlockSpec((tm, d), lambda i: (i, 0)),
        out_shape=jax.ShapeDtypeStruct((t, d), BF16),
        compiler_params=pltpu.CompilerParams(
            dimension_semantics=("arbitrary",),
            vmem_limit_bytes=_vmem_limit(4 * _nbytes((tm, d), F32), 2 * _nbytes((tm, d), BF16),
                                         4 * 1024 * 1024)),
        name="rmsnorm",
    )(x, w.reshape(1, d).astype(F32))


def _matmul_kernel(*refs, n_pairs, has_res, has_gain, ssq_dim, act, nk):
    refs = list(refs)
    lhs = [refs.pop(0) for _ in range(n_pairs)]
    rhs = [refs.pop(0) for _ in range(n_pairs)]
    res_ref = refs.pop(0) if has_res else None
    gain_ref = refs.pop(0) if has_gain else None
    ssq_in_ref = refs.pop(0) if ssq_dim else None
    o_ref = refs.pop(0)
    scaled_ref, ssq_out_ref = (refs.pop(0), refs.pop(0)) if has_gain else (None, None)
    acc_ref = refs.pop(0) if nk > 1 else None

    def product():
        part = jnp.dot(lhs[0][...], rhs[0][...], preferred_element_type=F32)
        for a, b in zip(lhs[1:], rhs[1:]):
            part = part + jnp.dot(a[...], b[...], preferred_element_type=F32)
        return part

    def epilogue(v):
        if ssq_dim:
            v = v * lax.rsqrt(ssq_in_ref[:, :1] * (1.0 / ssq_dim) + EPS)
        if act == "relu2":
            v = jnp.square(jnp.maximum(v, 0.0))
        if has_res:
            v = v + res_ref[...]
        o_ref[...] = v.astype(o_ref.dtype)
        if has_gain:
            scaled_ref[...] = (v * gain_ref[...]).astype(scaled_ref.dtype)
            row_ss = jnp.broadcast_to(jnp.sum(v * v, axis=-1, keepdims=True), ssq_out_ref.shape)
            j = pl.program_id(1)

            @pl.when(j == 0)
            def _():
                ssq_out_ref[...] = row_ss

            @pl.when(j > 0)
            def _():
                ssq_out_ref[...] += row_ss

    if nk == 1:
        epilogue(product())
        return

    k = pl.program_id(2)

    @pl.when(k == 0)
    def _():
        acc_ref[...] = product()

    @pl.when(jnp.logical_and(k > 0, k < nk - 1))
    def _():
        acc_ref[...] += product()

    @pl.when(k == nk - 1)
    def _():
        epilogue(acc_ref[...] + product())


def _matmul(lhs_list, rhs_list, *, residual=None, act=None, next_norm_gain=None, lhs_row_ssq=None,
            n_out=None, out_dtype, tm, tn, tk=None, name):
    m, kdim = lhs_list[0].shape
    n = rhs_list[0].shape[1] if n_out is None else n_out
    tk = kdim if tk is None else tk
    nk = kdim // tk
    assert m % tm == 0 and n % tn == 0 and kdim % tk == 0
    n_pairs = len(lhs_list)
    in_specs = ([pl.BlockSpec((tm, tk), lambda i, j, k: (i, k))] * n_pairs
                + [pl.BlockSpec((tk, tn), lambda i, j, k: (k, j))] * n_pairs)
    args = list(lhs_list) + list(rhs_list)
    vmem = [2 * n_pairs * (_nbytes((tm, tk), BF16) + _nbytes((tk, tn), BF16)),
            2 * _nbytes((tm, tn), out_dtype),
            2 * _nbytes((tm, tn), F32)]
    tile_spec = pl.BlockSpec((tm, tn), lambda i, j, k: (i, j))
    row_spec = pl.BlockSpec((tm, V7X_LANES), lambda i, j, k: (i, 0))
    out_specs = [tile_spec]
    out_shape = [jax.ShapeDtypeStruct((m, n), out_dtype)]
    if residual is not None:
        in_specs.append(tile_spec)
        args.append(residual)
        vmem.append(2 * _nbytes((tm, tn), residual.dtype))
    if next_norm_gain is not None:
        assert nk == 1
        in_specs.append(pl.BlockSpec((1, tn), lambda i, j, k: (0, j)))
        args.append(next_norm_gain.reshape(1, n).astype(F32))
        out_specs += [tile_spec, row_spec]
        out_shape += [jax.ShapeDtypeStruct((m, n), BF16), jax.ShapeDtypeStruct((m, V7X_LANES), F32)]
        vmem.append(2 * _nbytes((tm, tn), BF16) + 2 * _nbytes((tm, tn), F32)
                    + 2 * _nbytes((tm, V7X_LANES), F32))
    if lhs_row_ssq is not None:
        in_specs.append(row_spec)
        args.append(lhs_row_ssq)
        vmem.append(2 * _nbytes((tm, V7X_LANES), F32))
    scratch = []
    if nk > 1:
        scratch.append(pltpu.VMEM((tm, tn), F32))
        vmem.append(_nbytes((tm, tn), F32))
    outs = pl.pallas_call(
        functools.partial(_matmul_kernel, n_pairs=n_pairs, has_res=residual is not None,
                          has_gain=next_norm_gain is not None,
                          ssq_dim=kdim if lhs_row_ssq is not None else 0, act=act, nk=nk),
        grid=(m // tm, n // tn, nk),
        in_specs=in_specs,
        out_specs=out_specs,
        out_shape=out_shape,
        scratch_shapes=scratch,
        compiler_params=pltpu.CompilerParams(
            dimension_semantics=("arbitrary", "arbitrary", "arbitrary"),
            vmem_limit_bytes=_vmem_limit(*vmem)),
        name=name,
    )(*args)
    return outs if next_norm_gain is not None else outs[0]


def _hgrn2_kernel(q_ref, f_ref, i_ref, g_ref, lb_ref, nw_ref, o_ref, state_ref, *, n_chunks, n_heads):
    c = HG_CHUNK

    @pl.when(pl.program_id(2) == 0)
    def _():
        state_ref[...] = jnp.zeros_like(state_ref)

    lb_raw = lb_ref[...]
    e = jnp.exp(lb_raw - jnp.max(lb_raw, axis=0, keepdims=True))
    lb_all = e[0:1, :] / jnp.sum(e, axis=0, keepdims=True)
    nw = nw_ref[...]

    row = lax.broadcasted_iota(jnp.int32, (c, c), 0)
    col = lax.broadcasted_iota(jnp.int32, (c, c), 1)
    causal = row >= col
    tri = jnp.where(causal, 1.0, 0.0).astype(BF16)

    nt_dims = (((1,), (1,)), ((), ()))
    tn_dims = (((0,), (0,)), ((), ()))
    head_cols = [slice(hd * HG_DK, (hd + 1) * HG_DK) for hd in range(n_heads)]

    def chunk(ci, carry):
        sl = pl.ds(pl.multiple_of(ci * c, c), c)
        q_raw = q_ref[sl, :]
        f_raw = f_ref[sl, :]
        q = q_raw / (1.0 + jnp.exp(-q_raw))
        f = lb_all + (1.0 - lb_all) / (1.0 + jnp.exp(-f_raw))
        k = 1.0 - f
        logf = jnp.log(f)
        hi = logf.astype(BF16)
        lo = (logf - hi.astype(F32)).astype(BF16)
        b = (jnp.dot(tri, hi, preferred_element_type=F32)
             + jnp.dot(tri, lo, preferred_element_type=F32))
        b_last = b[c - 1:c, :]
        q_dec = (q * jnp.exp(b)).astype(BF16)
        k_inv = (k * jnp.exp(-b)).astype(BF16)
        k_tail = (k * jnp.exp(b_last - b)).astype(BF16)
        decay = jnp.exp(b_last)
        vb = i_ref[sl, :].astype(BF16)

        a = [lax.dot_general(q_dec[:, s], k_inv[:, s], nt_dims, preferred_element_type=F32)
             for s in head_cols]
        a = [jnp.where(causal, x, 0.0).astype(BF16) for x in a]
        states = [state_ref[hd] for hd in range(n_heads)]
        o = [jnp.dot(a[hd], vb[:, s], preferred_element_type=F32)
             + lax.dot_general(q_dec[:, s], states[hd].astype(BF16), nt_dims, preferred_element_type=F32)
             for hd, s in enumerate(head_cols)]
        u = [lax.dot_general(vb[:, s], k_tail[:, s], tn_dims, preferred_element_type=F32)
             for s in head_cols]
        for hd, s in enumerate(head_cols):
            state_ref[hd] = states[hd] * decay[:, s] + u[hd]

        y = [x * lax.rsqrt(jnp.mean(x * x, axis=-1, keepdims=True) + EPS) * nw for x in o]
        g_raw = g_ref[sl, :]
        gate = g_raw / (1.0 + jnp.exp(-g_raw))
        o_ref[sl, :] = (jnp.concatenate(y, axis=1) * gate).astype(o_ref.dtype)
        return carry

    lax.fori_loop(0, n_chunks, chunk, 0, unroll=4)


def _hgrn2(proj_hg, hgrn_lb, out_norm_w, *, batch, seq, ct, heads_per_step):
    t = batch * seq
    hg = heads_per_step
    n_groups = HG_HEADS // hg
    nt = seq // ct
    assert seq % ct == 0 and ct % HG_CHUNK == 0 and HG_HEADS % hg == 0

    def col_spec(section):
        return pl.BlockSpec((ct, hg * HG_DK), lambda b, gg, tt: (b * nt + tt, section * n_groups + gg))

    blk = _nbytes((ct, hg * HG_DK), F32)
    return pl.pallas_call(
        functools.partial(_hgrn2_kernel, n_chunks=ct // HG_CHUNK, n_heads=hg),
        grid=(batch, n_groups, nt),
        in_specs=[col_spec(0), col_spec(1), col_spec(2), col_spec(3),
                  pl.BlockSpec((hgrn_lb.shape[0], hg * HG_DK), lambda b, gg, tt: (0, gg)),
                  pl.BlockSpec((1, HG_DV), lambda b, gg, tt: (0, 0))],
        out_specs=pl.BlockSpec((ct, hg * HG_DV), lambda b, gg, tt: (b * nt + tt, gg)),
        out_shape=jax.ShapeDtypeStruct((t, HG_HEADS * HG_DV), BF16),
        scratch_shapes=[pltpu.VMEM((hg, HG_DV, HG_DK), F32)],
        compiler_params=pltpu.CompilerParams(
            dimension_semantics=("arbitrary", "arbitrary", "arbitrary"),
            vmem_limit_bytes=_vmem_limit(8 * blk, 2 * _nbytes((ct, hg * HG_DV), BF16),
                                         8 * 1024 * 1024)),
        name="hgrn2",
    )(proj_hg, proj_hg, proj_hg, proj_hg, hgrn_lb.astype(F32), out_norm_w.reshape(1, HG_DV).astype(F32))


def _mla_prep_kernel(p_ref, pos_ref, qnw_ref, wuq_ref, kvnw_ref, wukv_ref, qhw_ref, khw_ref,
                     invf_ref, sgn_ref, qt_out, k_out, vt_out, *, scale):
    p = p_ref[...]
    c_q = p[:, :Q_RANK]
    c_kv = p[:, Q_RANK:Q_RANK + KV_RANK]
    k_pe = p[:, Q_RANK + KV_RANK:]

    def rms(x, w):
        return x * lax.rsqrt(jnp.mean(x * x, axis=-1, keepdims=True) + EPS) * w

    q_all = jnp.dot(rms(c_q, qnw_ref[...]).astype(BF16), wuq_ref[...], preferred_element_type=F32)
    kv_all = jnp.dot(rms(c_kv, kvnw_ref[...]).astype(BF16), wukv_ref[...], preferred_element_type=F32)

    ang = pos_ref[...].astype(F32) * invf_ref[...]
    cos = jnp.cos(ang)
    sin = jnp.sin(ang) * sgn_ref[...]

    def rope(x):
        return x * cos + pltpu.roll(x, V7X_LANES // 2, 1) * sin

    qhw = qhw_ref[...]
    khw = khw_ref[...]
    inv_d = 1.0 / QK_D
    k_pe_ss = jnp.sum(k_pe * k_pe, axis=-1, keepdims=True)
    k_pe_rot = rope(k_pe * khw[:, V7X_LANES:])

    for h in range(MLA_HEADS):
        qa = q_all[:, h * QK_PAD:h * QK_PAD + V7X_LANES]
        qb = q_all[:, h * QK_PAD + V7X_LANES:(h + 1) * QK_PAD]
        ss = jnp.sum(qa * qa, axis=-1, keepdims=True) + jnp.sum(qb * qb, axis=-1, keepdims=True)
        r = lax.rsqrt(ss * inv_d + EPS) * scale
        qt_out[0, h, :V7X_LANES, :] = (qa * r * qhw[:, :V7X_LANES]).T.astype(qt_out.dtype)
        qt_out[0, h, V7X_LANES:, :] = rope(qb * r * qhw[:, V7X_LANES:]).T.astype(qt_out.dtype)

        ka = kv_all[:, h * 2 * V7X_LANES:h * 2 * V7X_LANES + NOPE_D]
        vv = kv_all[:, h * 2 * V7X_LANES + NOPE_D:(h + 1) * 2 * V7X_LANES]
        ssk = jnp.sum(ka * ka, axis=-1, keepdims=True) + k_pe_ss
        rk = lax.rsqrt(ssk * inv_d + EPS)
        k_out[0, h, :, :V7X_LANES] = (ka * rk * khw[:, :V7X_LANES]).astype(k_out.dtype)
        k_out[0, h, :, V7X_LANES:] = (k_pe_rot * rk).astype(k_out.dtype)
        vt_out[0, h, 0, :V_D, :] = vv.T.astype(vt_out.dtype)
        vt_out[0, h, 0, V_D:, :] = jnp.ones((VT_ROWS - V_D, vv.shape[0]), vt_out.dtype)


def _pad_rope_cols(w):
    z = jnp.zeros(w.shape[:-1] + (ROPE_HALF,), w.dtype)
    return jnp.concatenate([w[..., :ROPE_HALF], z, w[..., ROPE_HALF:], z], axis=-1)


def _pad_head_cols(w):
    return jnp.concatenate([w[..., :NOPE_D], _pad_rope_cols(w[..., NOPE_D:])], axis=-1)


def _mla_prep(proj_mla, positions, q_norm_w, w_uq, kv_norm_w, w_ukv, q_head_norm_w, k_head_norm_w,
              *, batch, seq, tm):
    t = batch * seq
    nh = MLA_HEADS
    pw = proj_mla.shape[1]
    wuq_p = _pad_head_cols(w_uq.reshape(Q_RANK, nh, QK_D)).reshape(Q_RANK, nh * QK_PAD).astype(BF16)
    wukv = w_ukv.astype(BF16)
    qhw = _pad_head_cols(q_head_norm_w.reshape(1, QK_D)).astype(F32)
    khw = _pad_head_cols(k_head_norm_w.reshape(1, QK_D)).astype(F32)
    inv_freq = ROPE_THETA ** (-jnp.arange(0, ROPE_D, 2, dtype=F32) / ROPE_D)
    invf = _pad_rope_cols(jnp.concatenate([inv_freq, inv_freq]).reshape(1, ROPE_D))
    sgn = _pad_rope_cols(jnp.concatenate([-jnp.ones((ROPE_HALF,), F32),
                                          jnp.ones((ROPE_HALF,), F32)]).reshape(1, ROPE_D))
    nst = seq // tm
    const = lambda i: (0, 0)
    vmem = [2 * _nbytes((tm, pw), F32), 2 * _nbytes(wuq_p.shape, BF16), 2 * _nbytes(wukv.shape, BF16),
            2 * 2 * _nbytes((nh, tm, QK_PAD), BF16), 2 * _nbytes((nh, tm, VT_ROWS), BF16),
            4 * _nbytes((tm, nh * QK_PAD), F32), 4 * 1024 * 1024]
    scale = math.log2(math.e) / math.sqrt(QK_D)
    return pl.pallas_call(
        functools.partial(_mla_prep_kernel, scale=scale),
        grid=(t // tm,),
        in_specs=[pl.BlockSpec((tm, pw), lambda i: (i, 0)),
                  pl.BlockSpec((tm, 1), lambda i: (i, 0)),
                  pl.BlockSpec((1, Q_RANK), const),
                  pl.BlockSpec(wuq_p.shape, const),
                  pl.BlockSpec((1, KV_RANK), const),
                  pl.BlockSpec(wukv.shape, const),
                  pl.BlockSpec((1, QK_PAD), const),
                  pl.BlockSpec((1, QK_PAD), const),
                  pl.BlockSpec((1, V7X_LANES), const),
                  pl.BlockSpec((1, V7X_LANES), const)],
        out_specs=[pl.BlockSpec((1, nh, QK_PAD, tm), lambda i: (i // nst, 0, 0, i % nst)),
                   pl.BlockSpec((1, nh, tm, QK_PAD), lambda i: (i // nst, 0, i % nst, 0)),
                   pl.BlockSpec((1, nh, 1, VT_ROWS, tm), lambda i: (i // nst, 0, i % nst, 0, 0))],
        out_shape=[jax.ShapeDtypeStruct((batch, nh, QK_PAD, seq), BF16),
                   jax.ShapeDtypeStruct((batch, nh, seq, QK_PAD), BF16),
                   jax.ShapeDtypeStruct((batch, nh, nst, VT_ROWS, tm), BF16)],
        compiler_params=pltpu.CompilerParams(
            dimension_semantics=("arbitrary",),
            vmem_limit_bytes=_vmem_limit(*vmem)),
        name="mla_prep",
    )(proj_mla, positions.reshape(t, 1), q_norm_w.reshape(1, Q_RANK).astype(F32), wuq_p,
      kv_norm_w.reshape(1, KV_RANK).astype(F32), wukv, qhw, khw, invf, sgn)


def _flash_kernel(qt_ref, k_ref, vt_ref, o_ref, acc_ref, st_a, st_b, p_a, p_b, al_a, al_b, cm_a, cm_b,
                  *, tq, tk):
    qi = pl.program_id(2)
    n_pairs = (tq // tk // 2) * qi
    n_full = 2 * n_pairs
    vt_tile = vt_ref.shape[-1]
    n_vt = tk // vt_tile

    def scores(blk, st_ref, cm_ref, cols, masked):
        kj = k_ref[0, 0, pl.ds(pl.multiple_of(blk * tk, tk), tk), :]
        st = jnp.dot(kj, qt_ref[0, 0, :, cols], preferred_element_type=F32)
        if masked:
            st = causal_mask(blk, st, cols)
        st_ref[:, cols] = st
        cm_ref[:, cols] = jnp.max(st, axis=0, keepdims=True)

    def causal_mask(blk, st, cols):
        n_cols = cols.stop - cols.start
        k_idx = blk * tk + lax.broadcasted_iota(jnp.int32, (tk, n_cols), 0)
        q_idx = qi * tq + cols.start + lax.broadcasted_iota(jnp.int32, (tk, n_cols), 1)
        return jnp.where(k_idx <= q_idx, st, -jnp.inf)

    def remask(blk, st_ref, cm_ref, cols):
        st = causal_mask(blk, st_ref[:, cols], cols)
        st_ref[:, cols] = st
        cm_ref[:, cols] = jnp.max(st, axis=0, keepdims=True)

    def softmax(st_ref, cm_ref, p_ref, al_ref, m_prev, cols):
        m_new = jnp.maximum(m_prev, cm_ref[:, cols])
        p_ref[:, cols] = jnp.exp2(st_ref[:, cols] - m_new).astype(BF16)
        al_ref[:, cols] = jnp.exp2(m_prev - m_new)
        return m_new

    def accumulate(blk, p_ref, al_ref, cols):
        vt = jnp.concatenate([vt_ref[0, 0, blk * n_vt + t] for t in range(n_vt)], axis=1)
        acc_ref[:, cols] = al_ref[:, cols] * acc_ref[:, cols] + jnp.dot(
            vt, p_ref[:, cols], preferred_element_type=F32)

    acc_ref[...] = jnp.zeros_like(acc_ref)
    p_b[...] = jnp.zeros_like(p_b)
    al_b[...] = jnp.ones_like(al_b)
    m_run = jnp.full((1, tq), -jnp.inf, F32)

    n_chunk = tq // (2 * V7X_LANES)
    chunks = [slice(c * 2 * V7X_LANES, (c + 1) * 2 * V7X_LANES) for c in range(n_chunk)]
    for cols in chunks:
        scores(0, st_a, cm_a, cols, False)

    def pipelined(blk_next, st_next, cm_next, st_cur, cm_cur, p_cur, al_cur, blk_prev, p_prev, al_prev, m):
        m_parts = []
        for cols in chunks:
            scores(blk_next, st_next, cm_next, cols, False)
            m_parts.append(softmax(st_cur, cm_cur, p_cur, al_cur, m[:, cols], cols))
            accumulate(blk_prev, p_prev, al_prev, cols)
        return jnp.concatenate(m_parts, axis=1)

    def body(i, m):
        h = 2 * i
        m = pipelined(h + 1, st_b, cm_b, st_a, cm_a, p_a, al_a, jnp.maximum(h - 1, 0), p_b, al_b, m)
        return pipelined(h + 2, st_a, cm_a, st_b, cm_b, p_b, al_b, h, p_a, al_a, m)

    m_run = lax.fori_loop(0, n_pairs, body, m_run)
    dv = o_ref.shape[-1]
    sets = ((st_a, cm_a, p_a, al_a), (st_b, cm_b, p_b, al_b))
    for cols in chunks:
        own_block = cols.start // tk
        accumulate(jnp.maximum(n_full - 1, 0), p_b, al_b, cols)
        m_cols = m_run[:, cols]
        for d in range(own_block + 1):
            st_x, cm_x, p_x, al_x = sets[d % 2]
            triangular = d == own_block
            if d > 0:
                scores(n_full + d, st_x, cm_x, cols, triangular)
            elif triangular:
                remask(n_full, st_x, cm_x, cols)
            m_cols = softmax(st_x, cm_x, p_x, al_x, m_cols, cols)
            accumulate(n_full + d, p_x, al_x, cols)
        o_ref[0, cols, :] = (acc_ref[:dv, cols] / acc_ref[dv:dv + 1, cols]).T.astype(o_ref.dtype)


def _flash_attention(qt, k, vt, *, tq, tk):
    b, h, dqk, s = qt.shape
    nvb, vt_rows, tv = vt.shape[2:]
    dv = V_D
    assert tk % tv == 0 and s % tq == 0 and tq % (2 * tk) == 0 and vt_rows == VT_ROWS
    vmem = [2 * _nbytes((dqk, tq), BF16), 2 * _nbytes((s, dqk), BF16), 2 * _nbytes((nvb, vt_rows, tv), BF16),
            2 * _nbytes((tq, dv), BF16), 3 * _nbytes((vt_rows, tq), F32),
            2 * _nbytes((tk, tq), F32), 2 * _nbytes((tk, tq), BF16),
            3 * _nbytes((tk, tq), F32), 4 * 1024 * 1024]
    return pl.pallas_call(
        functools.partial(_flash_kernel, tq=tq, tk=tk),
        grid=(b, h, s // tq),
        in_specs=[pl.BlockSpec((1, 1, dqk, tq), lambda bb, hh, qi: (bb, hh, 0, qi)),
                  pl.BlockSpec((1, 1, s, dqk), lambda bb, hh, qi: (bb, hh, 0, 0)),
                  pl.BlockSpec((1, 1, nvb, vt_rows, tv), lambda bb, hh, qi: (bb, hh, 0, 0, 0))],
        out_specs=pl.BlockSpec((1, tq, dv), lambda bb, hh, qi: (bb, qi, hh)),
        out_shape=jax.ShapeDtypeStruct((b, s, h * dv), BF16),
        scratch_shapes=[pltpu.VMEM((vt_rows, tq), F32),
                        pltpu.VMEM((tk, tq), F32), pltpu.VMEM((tk, tq), F32),
                        pltpu.VMEM((tk, tq), BF16), pltpu.VMEM((tk, tq), BF16),
                        pltpu.VMEM((1, tq), F32), pltpu.VMEM((1, tq), F32),
                        pltpu.VMEM((1, tq), F32), pltpu.VMEM((1, tq), F32)],
        compiler_params=pltpu.CompilerParams(
            dimension_semantics=("arbitrary", "arbitrary", "arbitrary"),
            vmem_limit_bytes=_vmem_limit(*vmem)),
        name="mla_flash",
    )(qt, k, vt)


def kernel(x, positions, norm1_w, w_in, hgrn_lb, hgrn_out_norm_w, mla_q_norm_w, w_uq, mla_kv_norm_w,
           w_ukv, q_head_norm_w, k_head_norm_w, w_o, norm2_w, w_up, w_down):
    batch, seq, d = x.shape
    t = batch * seq
    depth = norm1_w.shape[0]
    hg_w = 2 * HG_HEADS * HG_DK + 2 * HG_HEADS * HG_DV
    h = x.reshape(t, d)
    for l in range(depth):
        n1 = _rmsnorm(h, norm1_w[l])
        w_in_l = w_in[l].astype(BF16)
        w_mla = jnp.concatenate([w_in_l[:, hg_w:hg_w + Q_RANK + KV_RANK],
                                 _pad_rope_cols(w_in_l[:, hg_w + Q_RANK + KV_RANK:])], axis=-1)
        proj_hg = _matmul([n1], [w_in_l], n_out=hg_w, out_dtype=F32, tm=1024, tn=1024, name="in_proj_hg")
        proj_mla = _matmul([n1], [w_mla], out_dtype=F32, tm=512, tn=w_mla.shape[1], name="in_proj_mla")

        assert depth == 1
        o_hg = _hgrn2(proj_hg, hgrn_lb, hgrn_out_norm_w[l], batch=batch, seq=seq, ct=512,
                      heads_per_step=8)

        qt, k, vt = _mla_prep(proj_mla, positions, mla_q_norm_w[l], w_uq[l], mla_kv_norm_w[l], w_ukv[l],
                              q_head_norm_w[l], k_head_norm_w[l], batch=batch, seq=seq, tm=256)
        o_mla = _flash_attention(qt, k, vt, tq=1024, tk=512).reshape(t, MLA_HEADS * V_D)

        w_o_l = w_o[l].astype(BF16)
        hg_vw = HG_HEADS * HG_DV
        h, h_gained, h_ssq = _matmul([o_hg, o_mla], [w_o_l[:hg_vw], w_o_l[hg_vw:]], residual=h,
                                     next_norm_gain=norm2_w[l], out_dtype=F32, tm=512, tn=1024,
                                     name="out_proj")
        hid = _matmul([h_gained], [w_up[l].astype(BF16)], lhs_row_ssq=h_ssq, act="relu2", out_dtype=BF16,
                      tm=1024, tn=1024, name="mlp_up")
        h = _matmul([hid], [w_down[l].astype(BF16)], residual=h, out_dtype=F32, tm=1024, tn=1024,
                    tk=2048, name="mlp_down")
    return h.reshape(batch, seq, d)
```

```python
import functools
import math

import jax
import jax.numpy as jnp
from jax import lax
from jax.experimental import pallas as pl
from jax.experimental.pallas import tpu as pltpu

F32 = jnp.float32
BF16 = jnp.bfloat16

HG_HEADS = 16
HG_DK = 128
HG_DV = 128
HG_CHUNK = 64
MLA_HEADS = 16
Q_RANK = 768
KV_RANK = 512
NOPE_D = 128
ROPE_D = 64
QK_D = NOPE_D + ROPE_D
V_D = 128
ROPE_THETA = 10000.0
EPS = 1e-6

V7X_LANES = 128
V7X_BF16_SUBLANES = 16
V7X_VMEM_LIMIT_CAP = 56 * 1024 * 1024

VT_ROWS = V_D + V7X_BF16_SUBLANES

QK_PAD = 2 * V7X_LANES
ROPE_HALF = ROPE_D // 2


def _vmem_limit(*byte_counts):
    return int(min(sum(byte_counts), V7X_VMEM_LIMIT_CAP))


def _nbytes(shape, dtype):
    return math.prod(shape) * jnp.dtype(dtype).itemsize


def _rmsnorm_kernel(x_ref, w_ref, o_ref):
    x = x_ref[...]
    ms = jnp.mean(x * x, axis=-1, keepdims=True)
    o_ref[...] = (x * lax.rsqrt(ms + EPS) * w_ref[...]).astype(o_ref.dtype)


def _rmsnorm(x, w, *, tm=256):
    t, d = x.shape
    return pl.pallas_call(
        _rmsnorm_kernel,
        grid=(t // tm,),
        in_specs=[pl.BlockSpec((tm, d), lambda i: (i, 0)),
                  pl.BlockSpec((1, d), lambda i: (0, 0))],
        out_specs=pl.BlockSpec((tm, d), lambda i: (i, 0)),
        out_shape=jax.ShapeDtypeStruct((t, d), BF16),
        compiler_params=pltpu.CompilerParams(
            dimension_semantics=("arbitrary",),
            vmem_limit_bytes=_vmem_limit(4 * _nbytes((tm, d), F32), 2 * _nbytes((tm, d), BF16),
                                         4 * 1024 * 1024)),
        name="rmsnorm",
    )(x, w.reshape(1, d).astype(F32))


def _matmul_kernel(*refs, n_pairs, has_res, has_gain, ssq_dim, act, nk):
    refs = list(refs)
    lhs = [refs.pop(0) for _ in range(n_pairs)]
    rhs = [refs.pop(0) for _ in range(n_pairs)]
    res_ref = refs.pop(0) if has_res else None
    gain_ref = refs.pop(0) if has_gain else None
    ssq_in_ref = refs.pop(0) if ssq_dim else None
    o_ref = refs.pop(0)
    scaled_ref, ssq_out_ref = (refs.pop(0), refs.pop(0)) if has_gain else (None, None)
    acc_ref = refs.pop(0) if refs else None

    def product():
        part = jnp.dot(lhs[0][...], rhs[0][...], preferred_element_type=F32)
        for a, b in zip(lhs[1:], rhs[1:]):
            part = part + jnp.dot(a[...], b[...], preferred_element_type=F32)
        return part

    def epilogue(v):
        if ssq_dim:
            v = v * lax.rsqrt(ssq_in_ref[:, :1] * (1.0 / ssq_dim) + EPS)
        if act == "relu2":
            v = jnp.square(jnp.maximum(v, 0.0))
        if has_res:
            v = v + res_ref[...]
        o_ref[...] = v.astype(o_ref.dtype)
        if has_gain:
            scaled_ref[...] = (v * gain_ref[...]).astype(scaled_ref.dtype)
            row_ss = jnp.broadcast_to(jnp.sum(v * v, axis=-1, keepdims=True), ssq_out_ref.shape)
            j = pl.program_id(1)

            @pl.when(j == 0)
            def _():
                ssq_out_ref[...] = row_ss

            @pl.when(j > 0)
            def _():
                ssq_out_ref[...] += row_ss

    if nk == 1:
        epilogue(product())
        return

    k = pl.program_id(2)

    if acc_ref is None:
        @pl.when(k == 0)
        def _():
            epilogue(product())

        @pl.when(k > 0)
        def _():
            o_ref[...] += product()

        return

    @pl.when(k == 0)
    def _():
        acc_ref[...] = product()

    @pl.when(jnp.logical_and(k > 0, k < nk - 1))
    def _():
        acc_ref[...] += product()

    @pl.when(k == nk - 1)
    def _():
        epilogue(acc_ref[...] + product())


def _matmul(lhs_list, rhs_list, *, residual=None, act=None, next_norm_gain=None, lhs_row_ssq=None,
            n_out=None, rhs_row_blocks=None, out_dtype, tm, tn, tk=None, name):
    m, kdim = lhs_list[0].shape
    n = rhs_list[0].shape[1] if n_out is None else n_out
    tk = kdim if tk is None else tk
    nk = kdim // tk
    assert m % tm == 0 and n % tn == 0 and kdim % tk == 0
    n_pairs = len(lhs_list)
    offsets = [0] * n_pairs if rhs_row_blocks is None else rhs_row_blocks
    in_specs = ([pl.BlockSpec((tm, tk), lambda i, j, k: (i, k))] * n_pairs
                + [pl.BlockSpec((tk, tn), functools.partial(lambda i, j, k, off: (k + off, j), off=off))
                   for off in offsets])
    args = list(lhs_list) + list(rhs_list)
    vmem = [2 * n_pairs * (_nbytes((tm, tk), BF16) + _nbytes((tk, tn), BF16)),
            2 * _nbytes((tm, tn), out_dtype),
            2 * _nbytes((tm, tn), F32)]
    tile_spec = pl.BlockSpec((tm, tn), lambda i, j, k: (i, j))
    row_spec = pl.BlockSpec((tm, V7X_LANES), lambda i, j, k: (i, 0))
    out_specs = [tile_spec]
    out_shape = [jax.ShapeDtypeStruct((m, n), out_dtype)]
    if residual is not None:
        in_specs.append(tile_spec)
        args.append(residual)
        vmem.append(2 * _nbytes((tm, tn), residual.dtype))
    if next_norm_gain is not None:
        assert nk == 1
        in_specs.append(pl.BlockSpec((1, tn), lambda i, j, k: (0, j)))
        args.append(next_norm_gain.reshape(1, n).astype(F32))
        out_specs += [tile_spec, row_spec]
        out_shape += [jax.ShapeDtypeStruct((m, n), BF16), jax.ShapeDtypeStruct((m, V7X_LANES), F32)]
        vmem.append(2 * _nbytes((tm, tn), BF16) + 2 * _nbytes((tm, tn), F32)
                    + 2 * _nbytes((tm, V7X_LANES), F32))
    if lhs_row_ssq is not None:
        in_specs.append(row_spec)
        args.append(lhs_row_ssq)
        vmem.append(2 * _nbytes((tm, V7X_LANES), F32))
    scratch = []
    plain_f32_out = (out_dtype == F32 and act is None and next_norm_gain is None
                     and lhs_row_ssq is None)
    if nk > 1 and not plain_f32_out:
        scratch.append(pltpu.VMEM((tm, tn), F32))
        vmem.append(_nbytes((tm, tn), F32))
    outs = pl.pallas_call(
        functools.partial(_matmul_kernel, n_pairs=n_pairs, has_res=residual is not None,
                          has_gain=next_norm_gain is not None,
                          ssq_dim=kdim if lhs_row_ssq is not None else 0, act=act, nk=nk),
        grid=(m // tm, n // tn, nk),
        in_specs=in_specs,
        out_specs=out_specs,
        out_shape=out_shape,
        scratch_shapes=scratch,
        compiler_params=pltpu.CompilerParams(
            dimension_semantics=("arbitrary", "arbitrary", "arbitrary"),
            vmem_limit_bytes=_vmem_limit(*vmem)),
        name=name,
    )(*args)
    return outs if next_norm_gain is not None else outs[0]


def _hgrn2_kernel(q_ref, f_ref, i_ref, g_ref, lb_ref, nw_ref, o_ref, state_ref, *, n_chunks, n_heads):
    c = HG_CHUNK

    @pl.when(pl.program_id(2) == 0)
    def _():
        state_ref[...] = jnp.zeros_like(state_ref)

    lb_raw = lb_ref[...]
    e = jnp.exp(lb_raw - jnp.max(lb_raw, axis=0, keepdims=True))
    lb_all = e[0:1, :] / jnp.sum(e, axis=0, keepdims=True)
    nw = nw_ref[...]

    row = lax.broadcasted_iota(jnp.int32, (c, c), 0)
    col = lax.broadcasted_iota(jnp.int32, (c, c), 1)
    causal = row >= col
    tri = jnp.where(causal, 1.0, 0.0).astype(BF16)

    nt_dims = (((1,), (1,)), ((), ()))
    tn_dims = (((0,), (0,)), ((), ()))
    head_cols = [slice(hd * HG_DK, (hd + 1) * HG_DK) for hd in range(n_heads)]

    def chunk(ci, carry):
        sl = pl.ds(pl.multiple_of(ci * c, c), c)
        q_raw = q_ref[sl, :]
        f_raw = f_ref[sl, :]
        q = q_raw / (1.0 + jnp.exp(-q_raw))
        f = lb_all + (1.0 - lb_all) / (1.0 + jnp.exp(-f_raw))
        k = 1.0 - f
        logf = jnp.log(f)
        hi = logf.astype(BF16)
        lo = (logf - hi.astype(F32)).astype(BF16)
        b = (jnp.dot(tri, hi, preferred_element_type=F32)
             + jnp.dot(tri, lo, preferred_element_type=F32))
        b_last = b[c - 1:c, :]
        q_dec = (q * jnp.exp(b)).astype(BF16)
        decay = jnp.exp(b_last)
        k_inv_f32 = k * jnp.exp(-b)
        k_inv = k_inv_f32.astype(BF16)
        k_tail = (k_inv_f32 * decay).astype(BF16)
        vb = i_ref[sl, :].astype(BF16)

        a = [lax.dot_general(q_dec[:, s], k_inv[:, s], nt_dims, preferred_element_type=F32)
             for s in head_cols]
        a = [jnp.where(causal, x, 0.0).astype(BF16) for x in a]
        states = [state_ref[hd] for hd in range(n_heads)]
        o = [jnp.dot(a[hd], vb[:, s], preferred_element_type=F32)
             + lax.dot_general(q_dec[:, s], states[hd].astype(BF16), nt_dims, preferred_element_type=F32)
             for hd, s in enumerate(head_cols)]
        u = [lax.dot_general(vb[:, s], k_tail[:, s], tn_dims, preferred_element_type=F32)
             for s in head_cols]
        for hd, s in enumerate(head_cols):
            state_ref[hd] = states[hd] * decay[:, s] + u[hd]

        y = [x * lax.rsqrt(jnp.mean(x * x, axis=-1, keepdims=True) + EPS) * nw for x in o]
        g_raw = g_ref[sl, :]
        gate = g_raw / (1.0 + jnp.exp(-g_raw))
        o_ref[sl, :] = (jnp.concatenate(y, axis=1) * gate).astype(o_ref.dtype)
        return carry

    lax.fori_loop(0, n_chunks, chunk, 0, unroll=4)


def _hgrn2(proj_hg, hgrn_lb, out_norm_w, *, batch, seq, ct, heads_per_step):
    t = batch * seq
    hg = heads_per_step
    n_groups = HG_HEADS // hg
    nt = seq // ct
    assert seq % ct == 0 and ct % HG_CHUNK == 0 and HG_HEADS % hg == 0

    def col_spec(section):
        return pl.BlockSpec((ct, hg * HG_DK), lambda b, gg, tt: (b * nt + tt, section * n_groups + gg))

    blk = _nbytes((ct, hg * HG_DK), F32)
    return pl.pallas_call(
        functools.partial(_hgrn2_kernel, n_chunks=ct // HG_CHUNK, n_heads=hg),
        grid=(batch, n_groups, nt),
        in_specs=[col_spec(0), col_spec(1), col_spec(2), col_spec(3),
                  pl.BlockSpec((hgrn_lb.shape[0], hg * HG_DK), lambda b, gg, tt: (0, gg)),
                  pl.BlockSpec((1, HG_DV), lambda b, gg, tt: (0, 0))],
        out_specs=pl.BlockSpec((ct, hg * HG_DV), lambda b, gg, tt: (b * nt + tt, gg)),
        out_shape=jax.ShapeDtypeStruct((t, HG_HEADS * HG_DV), BF16),
        scratch_shapes=[pltpu.VMEM((hg, HG_DV, HG_DK), F32)],
        compiler_params=pltpu.CompilerParams(
            dimension_semantics=("arbitrary", "arbitrary", "arbitrary"),
            vmem_limit_bytes=_vmem_limit(8 * blk, 2 * _nbytes((ct, hg * HG_DV), BF16),
                                         8 * 1024 * 1024)),
        name="hgrn2",
    )(proj_hg, proj_hg, proj_hg, proj_hg, hgrn_lb.astype(F32), out_norm_w.reshape(1, HG_DV).astype(F32))


def _mla_prep_kernel(p_ref, pos_ref, qnw_ref, wuq_ref, kvnw_ref, wukv_ref, qhw_ref, khw_ref,
                     invf_ref, sgn_ref, qt_out, k_out, vt_out, *, scale):
    p = p_ref[...]
    c_q = p[:, :Q_RANK]
    c_kv = p[:, Q_RANK:Q_RANK + KV_RANK]
    k_pe = p[:, Q_RANK + KV_RANK:]

    def rms(x, w):
        return x * lax.rsqrt(jnp.mean(x * x, axis=-1, keepdims=True) + EPS) * w

    q_all = jnp.dot(rms(c_q, qnw_ref[...]).astype(BF16), wuq_ref[...], preferred_element_type=F32)
    kv_all = jnp.dot(rms(c_kv, kvnw_ref[...]).astype(BF16), wukv_ref[...], preferred_element_type=F32)

    ang = pos_ref[...].astype(F32) * invf_ref[...]
    cos = jnp.cos(ang)
    sin = jnp.sin(ang) * sgn_ref[...]

    def rope(x):
        return x * cos + pltpu.roll(x, V7X_LANES // 2, 1) * sin

    qhw = qhw_ref[...]
    khw = khw_ref[...]
    inv_d = 1.0 / QK_D
    k_pe_ss = jnp.sum(k_pe * k_pe, axis=-1, keepdims=True)
    k_pe_rot = rope(k_pe * khw[:, V7X_LANES:])

    for h in range(MLA_HEADS):
        qa = q_all[:, h * QK_PAD:h * QK_PAD + V7X_LANES]
        qb = q_all[:, h * QK_PAD + V7X_LANES:(h + 1) * QK_PAD]
        ss = jnp.sum(qa * qa, axis=-1, keepdims=True) + jnp.sum(qb * qb, axis=-1, keepdims=True)
        r = lax.rsqrt(ss * inv_d + EPS) * scale
        qt_out[0, h, :V7X_LANES, :] = (qa * r * qhw[:, :V7X_LANES]).T.astype(qt_out.dtype)
        qt_out[0, h, V7X_LANES:, :] = rope(qb * r * qhw[:, V7X_LANES:]).T.astype(qt_out.dtype)

        ka = kv_all[:, h * 2 * V7X_LANES:h * 2 * V7X_LANES + NOPE_D]
        vv = kv_all[:, h * 2 * V7X_LANES + NOPE_D:(h + 1) * 2 * V7X_LANES]
        ssk = jnp.sum(ka * ka, axis=-1, keepdims=True) + k_pe_ss
        rk = lax.rsqrt(ssk * inv_d + EPS)
        k_out[0, h, :, :V7X_LANES] = (ka * rk * khw[:, :V7X_LANES]).astype(k_out.dtype)
        k_out[0, h, :, V7X_LANES:] = (k_pe_rot * rk).astype(k_out.dtype)
        vt_out[0, h, 0, :V_D, :] = vv.T.astype(vt_out.dtype)
        vt_out[0, h, 0, V_D:, :] = jnp.ones((VT_ROWS - V_D, vv.shape[0]), vt_out.dtype)


def _pad_rope_cols(w):
    z = jnp.zeros(w.shape[:-1] + (ROPE_HALF,), w.dtype)
    return jnp.concatenate([w[..., :ROPE_HALF], z, w[..., ROPE_HALF:], z], axis=-1)


def _pad_head_cols(w):
    return jnp.concatenate([w[..., :NOPE_D], _pad_rope_cols(w[..., NOPE_D:])], axis=-1)


def _mla_prep(proj_mla, positions, q_norm_w, w_uq, kv_norm_w, w_ukv, q_head_norm_w, k_head_norm_w,
              *, batch, seq, tm):
    t = batch * seq
    nh = MLA_HEADS
    pw = proj_mla.shape[1]
    wuq_p = _pad_head_cols(w_uq.reshape(Q_RANK, nh, QK_D)).reshape(Q_RANK, nh * QK_PAD).astype(BF16)
    wukv = w_ukv.astype(BF16)
    qhw = _pad_head_cols(q_head_norm_w.reshape(1, QK_D)).astype(F32)
    khw = _pad_head_cols(k_head_norm_w.reshape(1, QK_D)).astype(F32)
    inv_freq = ROPE_THETA ** (-jnp.arange(0, ROPE_D, 2, dtype=F32) / ROPE_D)
    invf = _pad_rope_cols(jnp.concatenate([inv_freq, inv_freq]).reshape(1, ROPE_D))
    sgn = _pad_rope_cols(jnp.concatenate([-jnp.ones((ROPE_HALF,), F32),
                                          jnp.ones((ROPE_HALF,), F32)]).reshape(1, ROPE_D))
    nst = seq // tm
    const = lambda i: (0, 0)
    vmem = [2 * _nbytes((tm, pw), F32), 2 * _nbytes(wuq_p.shape, BF16), 2 * _nbytes(wukv.shape, BF16),
            2 * 2 * _nbytes((nh, tm, QK_PAD), BF16), 2 * _nbytes((nh, tm, VT_ROWS), BF16),
            4 * _nbytes((tm, nh * QK_PAD), F32), 4 * 1024 * 1024]
    scale = math.log2(math.e) / math.sqrt(QK_D)
    return pl.pallas_call(
        functools.partial(_mla_prep_kernel, scale=scale),
        grid=(t // tm,),
        in_specs=[pl.BlockSpec((tm, pw), lambda i: (i, 0)),
                  pl.BlockSpec((tm, 1), lambda i: (i, 0)),
                  pl.BlockSpec((1, Q_RANK), const),
                  pl.BlockSpec(wuq_p.shape, const),
                  pl.BlockSpec((1, KV_RANK), const),
                  pl.BlockSpec(wukv.shape, const),
                  pl.BlockSpec((1, QK_PAD), const),
                  pl.BlockSpec((1, QK_PAD), const),
                  pl.BlockSpec((1, V7X_LANES), const),
                  pl.BlockSpec((1, V7X_LANES), const)],
        out_specs=[pl.BlockSpec((1, nh, QK_PAD, tm), lambda i: (i // nst, 0, 0, i % nst)),
                   pl.BlockSpec((1, nh, tm, QK_PAD), lambda i: (i // nst, 0, i % nst, 0)),
                   pl.BlockSpec((1, nh, 1, VT_ROWS, tm), lambda i: (i // nst, 0, i % nst, 0, 0))],
        out_shape=[jax.ShapeDtypeStruct((batch, nh, QK_PAD, seq), BF16),
                   jax.ShapeDtypeStruct((batch, nh, seq, QK_PAD), BF16),
                   jax.ShapeDtypeStruct((batch, nh, nst, VT_ROWS, tm), BF16)],
        compiler_params=pltpu.CompilerParams(
            dimension_semantics=("arbitrary",),
            vmem_limit_bytes=_vmem_limit(*vmem)),
        name="mla_prep",
    )(proj_mla, positions.reshape(t, 1), q_norm_w.reshape(1, Q_RANK).astype(F32), wuq_p,
      kv_norm_w.reshape(1, KV_RANK).astype(F32), wukv, qhw, khw, invf, sgn)


def _flash_kernel(qt_ref, k_ref, vt_ref, o_ref, acc_ref, st_a, st_b, p_a, p_b, al_a, al_b, cm_a, cm_b,
                  *, tq, tk):
    qi = pl.program_id(2)
    n_pairs = (tq // tk // 2) * qi
    n_full = 2 * n_pairs
    vt_tile = vt_ref.shape[-1]
    n_vt = tk // vt_tile

    def scores(blk, st_ref, cm_ref, cols, masked):
        kj = k_ref[0, 0, pl.ds(pl.multiple_of(blk * tk, tk), tk), :]
        st = jnp.dot(kj, qt_ref[0, 0, :, cols], preferred_element_type=F32)
        if masked:
            st = causal_mask(blk, st, cols)
        st_ref[:, cols] = st
        cm_ref[:, cols] = jnp.max(st, axis=0, keepdims=True)

    def causal_mask(blk, st, cols):
        n_cols = cols.stop - cols.start
        k_idx = blk * tk + lax.broadcasted_iota(jnp.int32, (tk, n_cols), 0)
        q_idx = qi * tq + cols.start + lax.broadcasted_iota(jnp.int32, (tk, n_cols), 1)
        return jnp.where(k_idx <= q_idx, st, -jnp.inf)

    def remask(blk, st_ref, cm_ref, cols):
        st = causal_mask(blk, st_ref[:, cols], cols)
        st_ref[:, cols] = st
        cm_ref[:, cols] = jnp.max(st, axis=0, keepdims=True)

    def softmax(st_ref, cm_ref, p_ref, al_ref, m_prev, cols):
        m_new = jnp.maximum(m_prev, cm_ref[:, cols])
        p_ref[:, cols] = jnp.exp2(st_ref[:, cols] - m_new).astype(BF16)
        al_ref[:, cols] = jnp.exp2(m_prev - m_new)
        return m_new

    def accumulate(blk, p_ref, al_ref, cols):
        vt = jnp.concatenate([vt_ref[0, 0, blk * n_vt + t] for t in range(n_vt)], axis=1)
        acc_ref[:, cols] = al_ref[:, cols] * acc_ref[:, cols] + jnp.dot(
            vt, p_ref[:, cols], preferred_element_type=F32)

    acc_ref[...] = jnp.zeros_like(acc_ref)
    p_b[...] = jnp.zeros_like(p_b)
    al_b[...] = jnp.ones_like(al_b)
    m_run = jnp.full((1, tq), -jnp.inf, F32)

    n_chunk = tq // (2 * V7X_LANES)
    chunks = [slice(c * 2 * V7X_LANES, (c + 1) * 2 * V7X_LANES) for c in range(n_chunk)]
    for cols in chunks:
        scores(0, st_a, cm_a, cols, False)

    def pipelined(blk_next, st_next, cm_next, st_cur, cm_cur, p_cur, al_cur, blk_prev, p_prev, al_prev, m):
        m_parts = []
        for cols in chunks:
            scores(blk_next, st_next, cm_next, cols, False)
            m_parts.append(softmax(st_cur, cm_cur, p_cur, al_cur, m[:, cols], cols))
            accumulate(blk_prev, p_prev, al_prev, cols)
        return jnp.concatenate(m_parts, axis=1)

    def body(i, m):
        h = 2 * i
        m = pipelined(h + 1, st_b, cm_b, st_a, cm_a, p_a, al_a, jnp.maximum(h - 1, 0), p_b, al_b, m)
        return pipelined(h + 2, st_a, cm_a, st_b, cm_b, p_b, al_b, h, p_a, al_a, m)

    m_run = lax.fori_loop(0, n_pairs, body, m_run)
    dv = o_ref.shape[-1]
    sets = ((st_a, cm_a, p_a, al_a), (st_b, cm_b, p_b, al_b))
    for cols in chunks:
        own_block = cols.start // tk
        accumulate(jnp.maximum(n_full - 1, 0), p_b, al_b, cols)
        m_cols = m_run[:, cols]
        for d in range(own_block + 1):
            st_x, cm_x, p_x, al_x = sets[d % 2]
            triangular = d == own_block
            if d > 0:
                scores(n_full + d, st_x, cm_x, cols, triangular)
            elif triangular:
                remask(n_full, st_x, cm_x, cols)
            m_cols = softmax(st_x, cm_x, p_x, al_x, m_cols, cols)
            accumulate(n_full + d, p_x, al_x, cols)
        o_ref[0, cols, :] = (acc_ref[:dv, cols] / acc_ref[dv:dv + 1, cols]).T.astype(o_ref.dtype)


def _flash_attention(qt, k, vt, *, tq, tk):
    b, h, dqk, s = qt.shape
    nvb, vt_rows, tv = vt.shape[2:]
    dv = V_D
    assert tk % tv == 0 and s % tq == 0 and tq % (2 * tk) == 0 and vt_rows == VT_ROWS
    vmem = [2 * _nbytes((dqk, tq), BF16), 2 * _nbytes((s, dqk), BF16), 2 * _nbytes((nvb, vt_rows, tv), BF16),
            2 * _nbytes((tq, dv), BF16), 3 * _nbytes((vt_rows, tq), F32),
            2 * _nbytes((tk, tq), F32), 2 * _nbytes((tk, tq), BF16),
            3 * _nbytes((tk, tq), F32), 4 * 1024 * 1024]
    return pl.pallas_call(
        functools.partial(_flash_kernel, tq=tq, tk=tk),
        grid=(b, h, s // tq),
        in_specs=[pl.BlockSpec((1, 1, dqk, tq), lambda bb, hh, qi: (bb, hh, 0, qi)),
                  pl.BlockSpec((1, 1, s, dqk), lambda bb, hh, qi: (bb, hh, 0, 0)),
                  pl.BlockSpec((1, 1, nvb, vt_rows, tv), lambda bb, hh, qi: (bb, hh, 0, 0, 0))],
        out_specs=pl.BlockSpec((1, tq, dv), lambda bb, hh, qi: (bb, qi, hh)),
        out_shape=jax.ShapeDtypeStruct((b, s, h * dv), BF16),
        scratch_shapes=[pltpu.VMEM((vt_rows, tq), F32),
                        pltpu.VMEM((tk, tq), F32), pltpu.VMEM((tk, tq), F32),
                        pltpu.VMEM((tk, tq), BF16), pltpu.VMEM((tk, tq), BF16),
                        pltpu.VMEM((1, tq), F32), pltpu.VMEM((1, tq), F32),
                        pltpu.VMEM((1, tq), F32), pltpu.VMEM((1, tq), F32)],
        compiler_params=pltpu.CompilerParams(
            dimension_semantics=("arbitrary", "arbitrary", "arbitrary"),
            vmem_limit_bytes=_vmem_limit(*vmem)),
        name="mla_flash",
    )(qt, k, vt)


def kernel(x, positions, norm1_w, w_in, hgrn_lb, hgrn_out_norm_w, mla_q_norm_w, w_uq, mla_kv_norm_w,
           w_ukv, q_head_norm_w, k_head_norm_w, w_o, norm2_w, w_up, w_down):
    batch, seq, d = x.shape
    t = batch * seq
    depth = norm1_w.shape[0]
    hg_w = 2 * HG_HEADS * HG_DK + 2 * HG_HEADS * HG_DV
    h = x.reshape(t, d)
    for l in range(depth):
        n1 = _rmsnorm(h, norm1_w[l])
        w_in_l = w_in[l].astype(BF16)
        w_mla = jnp.concatenate([w_in_l[:, hg_w:hg_w + Q_RANK + KV_RANK],
                                 _pad_rope_cols(w_in_l[:, hg_w + Q_RANK + KV_RANK:])], axis=-1)
        proj_hg = _matmul([n1], [w_in_l], n_out=hg_w, out_dtype=F32, tm=1024, tn=1024, name="in_proj_hg")
        proj_mla = _matmul([n1], [w_mla], out_dtype=F32, tm=512, tn=w_mla.shape[1], name="in_proj_mla")

        assert depth == 1
        o_hg = _hgrn2(proj_hg, hgrn_lb, hgrn_out_norm_w[l], batch=batch, seq=seq, ct=512,
                      heads_per_step=8)

        qt, k, vt = _mla_prep(proj_mla, positions, mla_q_norm_w[l], w_uq[l], mla_kv_norm_w[l], w_ukv[l],
                              q_head_norm_w[l], k_head_norm_w[l], batch=batch, seq=seq, tm=256)
        o_mla = _flash_attention(qt, k, vt, tq=1024, tk=512).reshape(t, MLA_HEADS * V_D)

        w_o_l = w_o[l].astype(BF16)
        h, h_gained, h_ssq = _matmul([o_hg, o_mla], [w_o_l, w_o_l], rhs_row_blocks=[0, 1], residual=h,
                                     next_norm_gain=norm2_w[l], out_dtype=F32, tm=1024, tn=512,
                                     name="out_proj")
        hid = _matmul([h_gained], [w_up[l].astype(BF16)], lhs_row_ssq=h_ssq, act="relu2", out_dtype=BF16,
                      tm=1024, tn=1024, name="mlp_up")
        h = _matmul([hid], [w_down[l].astype(BF16)], residual=h, out_dtype=F32, tm=1024, tn=1024,
                    tk=4096, name="mlp_down")
    return h.reshape(batch, seq, d)
```

```python
import functools
import math

import jax
import jax.numpy as jnp
from jax import lax
from jax.experimental import pallas as pl
from jax.experimental.pallas import tpu as pltpu

F32 = jnp.float32
BF16 = jnp.bfloat16

HG_HEADS = 16
HG_DK = 128
HG_DV = 128
HG_CHUNK = 64
MLA_HEADS = 16
Q_RANK = 768
KV_RANK = 512
NOPE_D = 128
ROPE_D = 64
QK_D = NOPE_D + ROPE_D
V_D = 128
ROPE_THETA = 10000.0
EPS = 1e-6

V7X_LANES = 128
V7X_BF16_SUBLANES = 16
V7X_VMEM_LIMIT_CAP = 56 * 1024 * 1024

VT_ROWS = V_D + V7X_BF16_SUBLANES

QK_PAD = 2 * V7X_LANES
ROPE_HALF = ROPE_D // 2


def _vmem_limit(*byte_counts):
    return int(min(sum(byte_counts), V7X_VMEM_LIMIT_CAP))


def _nbytes(shape, dtype):
    return math.prod(shape) * jnp.dtype(dtype).itemsize


def _rmsnorm_proj_kernel(x_ref, g_ref, w_ref, n_ref, o_ref):
    x = x_ref[...]
    ms = jnp.mean(x * x, axis=-1, keepdims=True)
    n = (x * lax.rsqrt(ms + EPS) * g_ref[...]).astype(n_ref.dtype)
    n_ref[...] = n
    o_ref[...] = jnp.dot(n, w_ref[...], preferred_element_type=F32)


def _rmsnorm_proj(x, gain, w, *, tm):
    t, d = x.shape
    n = w.shape[1]
    return pl.pallas_call(
        _rmsnorm_proj_kernel,
        grid=(t // tm,),
        in_specs=[pl.BlockSpec((tm, d), lambda i: (i, 0)),
                  pl.BlockSpec((1, d), lambda i: (0, 0)),
                  pl.BlockSpec((d, n), lambda i: (0, 0))],
        out_specs=[pl.BlockSpec((tm, d), lambda i: (i, 0)),
                   pl.BlockSpec((tm, n), lambda i: (i, 0))],
        out_shape=[jax.ShapeDtypeStruct((t, d), BF16), jax.ShapeDtypeStruct((t, n), F32)],
        compiler_params=pltpu.CompilerParams(
            dimension_semantics=("arbitrary",),
            vmem_limit_bytes=_vmem_limit(4 * _nbytes((tm, d), F32), 2 * _nbytes((tm, d), BF16),
                                         2 * _nbytes((d, n), BF16), 3 * _nbytes((tm, n), F32),
                                         4 * 1024 * 1024)),
        name="norm_in_proj_mla",
    )(x, gain.reshape(1, d).astype(F32), w)


def _matmul_kernel(*refs, n_pairs, has_res, has_gain, ssq_dim, act, nk):
    refs = list(refs)
    lhs = [refs.pop(0) for _ in range(n_pairs)]
    rhs = [refs.pop(0) for _ in range(n_pairs)]
    res_ref = refs.pop(0) if has_res else None
    gain_ref = refs.pop(0) if has_gain else None
    ssq_in_ref = refs.pop(0) if ssq_dim else None
    o_ref = refs.pop(0)
    scaled_ref, ssq_out_ref = (refs.pop(0), refs.pop(0)) if has_gain else (None, None)
    acc_ref = refs.pop(0) if refs else None

    def product():
        part = jnp.dot(lhs[0][...], rhs[0][...], preferred_element_type=F32)
        for a, b in zip(lhs[1:], rhs[1:]):
            part = part + jnp.dot(a[...], b[...], preferred_element_type=F32)
        return part

    def epilogue(v):
        if ssq_dim:
            v = v * lax.rsqrt(ssq_in_ref[:, :1] * (1.0 / ssq_dim) + EPS)
        if act == "relu2":
            v = jnp.square(jnp.maximum(v, 0.0))
        if has_res:
            v = v + res_ref[...]
        o_ref[...] = v.astype(o_ref.dtype)
        if has_gain:
            scaled_ref[...] = (v * gain_ref[...]).astype(scaled_ref.dtype)
            row_ss = jnp.broadcast_to(jnp.sum(v * v, axis=-1, keepdims=True), ssq_out_ref.shape)
            j = pl.program_id(1)

            @pl.when(j == 0)
            def _():
                ssq_out_ref[...] = row_ss

            @pl.when(j > 0)
            def _():
                ssq_out_ref[...] += row_ss

    if nk == 1:
        epilogue(product())
        return

    k = pl.program_id(2)

    if acc_ref is None:
        @pl.when(k == 0)
        def _():
            epilogue(product())

        @pl.when(k > 0)
        def _():
            o_ref[...] += product()

        return

    @pl.when(k == 0)
    def _():
        acc_ref[...] = product()

    @pl.when(jnp.logical_and(k > 0, k < nk - 1))
    def _():
        acc_ref[...] += product()

    @pl.when(k == nk - 1)
    def _():
        epilogue(acc_ref[...] + product())


def _matmul(lhs_list, rhs_list, *, residual=None, act=None, next_norm_gain=None, lhs_row_ssq=None,
            n_out=None, rhs_row_blocks=None, out_dtype, tm, tn, tk=None, name):
    m, kdim = lhs_list[0].shape
    n = rhs_list[0].shape[1] if n_out is None else n_out
    tk = kdim if tk is None else tk
    nk = kdim // tk
    assert m % tm == 0 and n % tn == 0 and kdim % tk == 0
    n_pairs = len(lhs_list)
    offsets = [0] * n_pairs if rhs_row_blocks is None else rhs_row_blocks
    in_specs = ([pl.BlockSpec((tm, tk), lambda i, j, k: (i, k))] * n_pairs
                + [pl.BlockSpec((tk, tn), functools.partial(lambda i, j, k, off: (k + off, j), off=off))
                   for off in offsets])
    args = list(lhs_list) + list(rhs_list)
    vmem = [2 * n_pairs * (_nbytes((tm, tk), BF16) + _nbytes((tk, tn), BF16)),
            2 * _nbytes((tm, tn), out_dtype),
            2 * _nbytes((tm, tn), F32)]
    tile_spec = pl.BlockSpec((tm, tn), lambda i, j, k: (i, j))
    row_spec = pl.BlockSpec((tm, V7X_LANES), lambda i, j, k: (i, 0))
    out_specs = [tile_spec]
    out_shape = [jax.ShapeDtypeStruct((m, n), out_dtype)]
    if residual is not None:
        in_specs.append(tile_spec)
        args.append(residual)
        vmem.append(2 * _nbytes((tm, tn), residual.dtype))
    if next_norm_gain is not None:
        assert nk == 1
        in_specs.append(pl.BlockSpec((1, tn), lambda i, j, k: (0, j)))
        args.append(next_norm_gain.reshape(1, n).astype(F32))
        out_specs += [tile_spec, row_spec]
        out_shape += [jax.ShapeDtypeStruct((m, n), BF16), jax.ShapeDtypeStruct((m, V7X_LANES), F32)]
        vmem.append(2 * _nbytes((tm, tn), BF16) + 2 * _nbytes((tm, tn), F32)
                    + 2 * _nbytes((tm, V7X_LANES), F32))
    if lhs_row_ssq is not None:
        in_specs.append(row_spec)
        args.append(lhs_row_ssq)
        vmem.append(2 * _nbytes((tm, V7X_LANES), F32))
    scratch = []
    plain_f32_out = (out_dtype == F32 and act is None and next_norm_gain is None
                     and lhs_row_ssq is None)
    if nk > 1 and not plain_f32_out:
        scratch.append(pltpu.VMEM((tm, tn), F32))
        vmem.append(_nbytes((tm, tn), F32))
    outs = pl.pallas_call(
        functools.partial(_matmul_kernel, n_pairs=n_pairs, has_res=residual is not None,
                          has_gain=next_norm_gain is not None,
                          ssq_dim=kdim if lhs_row_ssq is not None else 0, act=act, nk=nk),
        grid=(m // tm, n // tn, nk),
        in_specs=in_specs,
        out_specs=out_specs,
        out_shape=out_shape,
        scratch_shapes=scratch,
        compiler_params=pltpu.CompilerParams(
            dimension_semantics=("arbitrary", "arbitrary", "arbitrary"),
            vmem_limit_bytes=_vmem_limit(*vmem)),
        name=name,
    )(*args)
    return outs if next_norm_gain is not None else outs[0]


def _hgrn2_kernel(q_ref, f_ref, i_ref, g_ref, lb_ref, nw_ref, o_ref, state_ref, *, n_chunks, n_heads):
    c = HG_CHUNK

    @pl.when(pl.program_id(2) == 0)
    def _():
        state_ref[...] = jnp.zeros_like(state_ref)

    lb_raw = lb_ref[...]
    e = jnp.exp(lb_raw - jnp.max(lb_raw, axis=0, keepdims=True))
    lb_all = e[0:1, :] / jnp.sum(e, axis=0, keepdims=True)
    nw = nw_ref[...]

    row = lax.broadcasted_iota(jnp.int32, (c, c), 0)
    col = lax.broadcasted_iota(jnp.int32, (c, c), 1)
    causal = row >= col
    tri = jnp.where(causal, 1.0, 0.0).astype(BF16)

    nt_dims = (((1,), (1,)), ((), ()))
    tn_dims = (((0,), (0,)), ((), ()))
    head_cols = [slice(hd * HG_DK, (hd + 1) * HG_DK) for hd in range(n_heads)]

    def chunk(ci, carry):
        sl = pl.ds(pl.multiple_of(ci * c, c), c)
        q_raw = q_ref[sl, :]
        f_raw = f_ref[sl, :]
        q = q_raw / (1.0 + jnp.exp(-q_raw))
        f = lb_all + (1.0 - lb_all) / (1.0 + jnp.exp(-f_raw))
        k = 1.0 - f
        logf = jnp.log(f)
        hi = logf.astype(BF16)
        lo = (logf - hi.astype(F32)).astype(BF16)
        b = (jnp.dot(tri, hi, preferred_element_type=F32)
             + jnp.dot(tri, lo, preferred_element_type=F32))
        b_last = b[c - 1:c, :]
        q_dec = (q * jnp.exp(b)).astype(BF16)
        decay = jnp.exp(b_last)
        k_inv_f32 = k * jnp.exp(-b)
        k_inv = k_inv_f32.astype(BF16)
        k_tail = (k_inv_f32 * decay).astype(BF16)
        vb = i_ref[sl, :].astype(BF16)

        a = [lax.dot_general(q_dec[:, s], k_inv[:, s], nt_dims, preferred_element_type=F32)
             for s in head_cols]
        a = [jnp.where(causal, x, 0.0).astype(BF16) for x in a]
        states = [state_ref[hd] for hd in range(n_heads)]
        o = [jnp.dot(a[hd], vb[:, s], preferred_element_type=F32)
             + lax.dot_general(q_dec[:, s], states[hd].astype(BF16), nt_dims, preferred_element_type=F32)
             for hd, s in enumerate(head_cols)]
        u = [lax.dot_general(vb[:, s], k_tail[:, s], tn_dims, preferred_element_type=F32)
             for s in head_cols]
        for hd, s in enumerate(head_cols):
            state_ref[hd] = states[hd] * decay[:, s] + u[hd]

        y = [x * lax.rsqrt(jnp.mean(x * x, axis=-1, keepdims=True) + EPS) * nw for x in o]
        g_raw = g_ref[sl, :]
        gate = g_raw / (1.0 + jnp.exp(-g_raw))
        o_ref[sl, :] = (jnp.concatenate(y, axis=1) * gate).astype(o_ref.dtype)
        return carry

    lax.fori_loop(0, n_chunks, chunk, 0, unroll=4)


def _hgrn2(proj_hg, hgrn_lb, out_norm_w, *, batch, seq, ct, heads_per_step):
    t = batch * seq
    hg = heads_per_step
    n_groups = HG_HEADS // hg
    nt = seq // ct
    assert seq % ct == 0 and ct % HG_CHUNK == 0 and HG_HEADS % hg == 0

    def col_spec(section):
        return pl.BlockSpec((ct, hg * HG_DK), lambda b, gg, tt: (b * nt + tt, section * n_groups + gg))

    blk = _nbytes((ct, hg * HG_DK), F32)
    return pl.pallas_call(
        functools.partial(_hgrn2_kernel, n_chunks=ct // HG_CHUNK, n_heads=hg),
        grid=(batch, n_groups, nt),
        in_specs=[col_spec(0), col_spec(1), col_spec(2), col_spec(3),
                  pl.BlockSpec((hgrn_lb.shape[0], hg * HG_DK), lambda b, gg, tt: (0, gg)),
                  pl.BlockSpec((1, HG_DV), lambda b, gg, tt: (0, 0))],
        out_specs=pl.BlockSpec((ct, hg * HG_DV), lambda b, gg, tt: (b * nt + tt, gg)),
        out_shape=jax.ShapeDtypeStruct((t, HG_HEADS * HG_DV), BF16),
        scratch_shapes=[pltpu.VMEM((hg, HG_DV, HG_DK), F32)],
        compiler_params=pltpu.CompilerParams(
            dimension_semantics=("arbitrary", "arbitrary", "arbitrary"),
            vmem_limit_bytes=_vmem_limit(8 * blk, 2 * _nbytes((ct, hg * HG_DV), BF16),
                                         8 * 1024 * 1024)),
        name="hgrn2",
    )(proj_hg, proj_hg, proj_hg, proj_hg, hgrn_lb.astype(F32), out_norm_w.reshape(1, HG_DV).astype(F32))


def _mla_prep_kernel(p_ref, pos_ref, qnw_ref, wuq_ref, kvnw_ref, wukv_ref, qhw_ref, khw_ref,
                     invf_ref, sgn_ref, qt_out, k_out, vt_out, *, scale):
    p = p_ref[...]
    c_q = p[:, :Q_RANK]
    c_kv = p[:, Q_RANK:Q_RANK + KV_RANK]
    k_pe = p[:, Q_RANK + KV_RANK:]

    def rms(x, w):
        return x * lax.rsqrt(jnp.mean(x * x, axis=-1, keepdims=True) + EPS) * w

    q_all = jnp.dot(rms(c_q, qnw_ref[...]).astype(BF16), wuq_ref[...], preferred_element_type=F32)
    kv_all = jnp.dot(rms(c_kv, kvnw_ref[...]).astype(BF16), wukv_ref[...], preferred_element_type=F32)

    ang = pos_ref[...].astype(F32) * invf_ref[...]
    cos = jnp.cos(ang)
    sin = jnp.sin(ang) * sgn_ref[...]

    def rope(x):
        return x * cos + pltpu.roll(x, V7X_LANES // 2, 1) * sin

    qhw = qhw_ref[...]
    khw = khw_ref[...]
    inv_d = 1.0 / QK_D
    k_pe_ss = jnp.sum(k_pe * k_pe, axis=-1, keepdims=True)
    k_pe_rot = rope(k_pe * khw[:, V7X_LANES:])

    for h in range(MLA_HEADS):
        qa = q_all[:, h * QK_PAD:h * QK_PAD + V7X_LANES]
        qb = q_all[:, h * QK_PAD + V7X_LANES:(h + 1) * QK_PAD]
        ss = jnp.sum(qa * qa, axis=-1, keepdims=True) + jnp.sum(qb * qb, axis=-1, keepdims=True)
        r = lax.rsqrt(ss * inv_d + EPS) * scale
        qt_out[0, h, :V7X_LANES, :] = (qa * r * qhw[:, :V7X_LANES]).T.astype(qt_out.dtype)
        qt_out[0, h, V7X_LANES:, :] = rope(qb * r * qhw[:, V7X_LANES:]).T.astype(qt_out.dtype)

        ka = kv_all[:, h * 2 * V7X_LANES:h * 2 * V7X_LANES + NOPE_D]
        vv = kv_all[:, h * 2 * V7X_LANES + NOPE_D:(h + 1) * 2 * V7X_LANES]
        ssk = jnp.sum(ka * ka, axis=-1, keepdims=True) + k_pe_ss
        rk = lax.rsqrt(ssk * inv_d + EPS)
        k_out[0, h, :, :V7X_LANES] = (ka * rk * khw[:, :V7X_LANES]).astype(k_out.dtype)
        k_out[0, h, :, V7X_LANES:] = (k_pe_rot * rk).astype(k_out.dtype)
        vt_out[0, h, 0, :V_D, :] = vv.T.astype(vt_out.dtype)
        vt_out[0, h, 0, V_D:, :] = jnp.ones((VT_ROWS - V_D, vv.shape[0]), vt_out.dtype)


def _pad_rope_cols(w):
    z = jnp.zeros(w.shape[:-1] + (ROPE_HALF,), w.dtype)
    return jnp.concatenate([w[..., :ROPE_HALF], z, w[..., ROPE_HALF:], z], axis=-1)


def _pad_head_cols(w):
    return jnp.concatenate([w[..., :NOPE_D], _pad_rope_cols(w[..., NOPE_D:])], axis=-1)


def _mla_prep(proj_mla, positions, q_norm_w, w_uq, kv_norm_w, w_ukv, q_head_norm_w, k_head_norm_w,
              *, batch, seq, tm):
    t = batch * seq
    nh = MLA_HEADS
    pw = proj_mla.shape[1]
    wuq_p = _pad_head_cols(w_uq.reshape(Q_RANK, nh, QK_D)).reshape(Q_RANK, nh * QK_PAD).astype(BF16)
    wukv = w_ukv.astype(BF16)
    qhw = _pad_head_cols(q_head_norm_w.reshape(1, QK_D)).astype(F32)
    khw = _pad_head_cols(k_head_norm_w.reshape(1, QK_D)).astype(F32)
    inv_freq = ROPE_THETA ** (-jnp.arange(0, ROPE_D, 2, dtype=F32) / ROPE_D)
    invf = _pad_rope_cols(jnp.concatenate([inv_freq, inv_freq]).reshape(1, ROPE_D))
    sgn = _pad_rope_cols(jnp.concatenate([-jnp.ones((ROPE_HALF,), F32),
                                          jnp.ones((ROPE_HALF,), F32)]).reshape(1, ROPE_D))
    nst = seq // tm
    const = lambda i: (0, 0)
    vmem = [2 * _nbytes((tm, pw), F32), 2 * _nbytes(wuq_p.shape, BF16), 2 * _nbytes(wukv.shape, BF16),
            2 * 2 * _nbytes((nh, tm, QK_PAD), BF16), 2 * _nbytes((nh, tm, VT_ROWS), BF16),
            4 * _nbytes((tm, nh * QK_PAD), F32), 4 * 1024 * 1024]
    scale = math.log2(math.e) / math.sqrt(QK_D)
    return pl.pallas_call(
        functools.partial(_mla_prep_kernel, scale=scale),
        grid=(t // tm,),
        in_specs=[pl.BlockSpec((tm, pw), lambda i: (i, 0)),
                  pl.BlockSpec((tm, 1), lambda i: (i, 0)),
                  pl.BlockSpec((1, Q_RANK), const),
                  pl.BlockSpec(wuq_p.shape, const),
                  pl.BlockSpec((1, KV_RANK), const),
                  pl.BlockSpec(wukv.shape, const),
                  pl.BlockSpec((1, QK_PAD), const),
                  pl.BlockSpec((1, QK_PAD), const),
                  pl.BlockSpec((1, V7X_LANES), const),
                  pl.BlockSpec((1, V7X_LANES), const)],
        out_specs=[pl.BlockSpec((1, nh, QK_PAD, tm), lambda i: (i // nst, 0, 0, i % nst)),
                   pl.BlockSpec((1, nh, tm, QK_PAD), lambda i: (i // nst, 0, i % nst, 0)),
                   pl.BlockSpec((1, nh, 1, VT_ROWS, tm), lambda i: (i // nst, 0, i % nst, 0, 0))],
        out_shape=[jax.ShapeDtypeStruct((batch, nh, QK_PAD, seq), BF16),
                   jax.ShapeDtypeStruct((batch, nh, seq, QK_PAD), BF16),
                   jax.ShapeDtypeStruct((batch, nh, nst, VT_ROWS, tm), BF16)],
        compiler_params=pltpu.CompilerParams(
            dimension_semantics=("arbitrary",),
            vmem_limit_bytes=_vmem_limit(*vmem)),
        name="mla_prep",
    )(proj_mla, positions.reshape(t, 1), q_norm_w.reshape(1, Q_RANK).astype(F32), wuq_p,
      kv_norm_w.reshape(1, KV_RANK).astype(F32), wukv, qhw, khw, invf, sgn)


def _flash_kernel(qt_ref, k_ref, vt_ref, o_ref, acc_ref, st_a, st_b, p_a, p_b, al_a, al_b, cm_a, cm_b,
                  *, tq, tk):
    qi = pl.program_id(2)
    n_pairs = (tq // tk // 2) * qi
    n_full = 2 * n_pairs
    vt_tile = vt_ref.shape[-1]
    n_vt = tk // vt_tile

    def scores(blk, st_ref, cm_ref, cols, masked):
        kj = k_ref[0, 0, pl.ds(pl.multiple_of(blk * tk, tk), tk), :]
        st = jnp.dot(kj, qt_ref[0, 0, :, cols], preferred_element_type=F32)
        if masked:
            st = causal_mask(blk, st, cols)
        st_ref[:, cols] = st
        cm_ref[:, cols] = jnp.max(st, axis=0, keepdims=True)

    def causal_mask(blk, st, cols):
        n_cols = cols.stop - cols.start
        k_idx = blk * tk + lax.broadcasted_iota(jnp.int32, (tk, n_cols), 0)
        q_idx = qi * tq + cols.start + lax.broadcasted_iota(jnp.int32, (tk, n_cols), 1)
        return jnp.where(k_idx <= q_idx, st, -jnp.inf)

    def remask(blk, st_ref, cm_ref, cols):
        st = causal_mask(blk, st_ref[:, cols], cols)
        st_ref[:, cols] = st
        cm_ref[:, cols] = jnp.max(st, axis=0, keepdims=True)

    def softmax(st_ref, cm_ref, p_ref, al_ref, m_prev, cols):
        m_new = jnp.maximum(m_prev, cm_ref[:, cols])
        p_ref[:, cols] = jnp.exp2(st_ref[:, cols] - m_new).astype(BF16)
        al_ref[:, cols] = jnp.exp2(m_prev - m_new)
        return m_new

    def accumulate(blk, p_ref, al_ref, cols):
        vt = jnp.concatenate([vt_ref[0, 0, blk * n_vt + t] for t in range(n_vt)], axis=1)
        acc_ref[:, cols] = al_ref[:, cols] * acc_ref[:, cols] + jnp.dot(
            vt, p_ref[:, cols], preferred_element_type=F32)

    n_chunk = tq // (2 * V7X_LANES)
    chunks = [slice(c * 2 * V7X_LANES, (c + 1) * 2 * V7X_LANES) for c in range(n_chunk)]
    for cols in chunks:
        scores(0, st_a, cm_a, cols, False)

    acc_ref[...] = jnp.zeros_like(acc_ref)
    p_b[...] = jnp.zeros_like(p_b)
    al_b[...] = jnp.ones_like(al_b)
    m_run = jnp.full((1, tq), -jnp.inf, F32)

    def pipelined(blk_next, st_next, cm_next, st_cur, cm_cur, p_cur, al_cur, blk_prev, p_prev, al_prev, m):
        m_parts = []
        for cols in chunks:
            scores(blk_next, st_next, cm_next, cols, False)
            accumulate(blk_prev, p_prev, al_prev, cols)
            m_parts.append(softmax(st_cur, cm_cur, p_cur, al_cur, m[:, cols], cols))
        return jnp.concatenate(m_parts, axis=1)

    def body(i, m):
        h = 2 * i
        m = pipelined(h + 1, st_b, cm_b, st_a, cm_a, p_a, al_a, jnp.maximum(h - 1, 0), p_b, al_b, m)
        return pipelined(h + 2, st_a, cm_a, st_b, cm_b, p_b, al_b, h, p_a, al_a, m)

    m_run = lax.fori_loop(0, n_pairs, body, m_run)
    dv = o_ref.shape[-1]
    sets = ((st_a, cm_a, p_a, al_a), (st_b, cm_b, p_b, al_b))
    for cols in chunks:
        own_block = cols.start // tk
        if own_block >= 1:
            scores(n_full + 1, st_b, cm_b, cols, own_block == 1)
        else:
            remask(n_full, st_a, cm_a, cols)
        accumulate(jnp.maximum(n_full - 1, 0), p_b, al_b, cols)
        m_cols = m_run[:, cols]
        for d in range(own_block + 1):
            st_x, cm_x, p_x, al_x = sets[d % 2]
            if d >= 2:
                scores(n_full + d, st_x, cm_x, cols, d == own_block)
            m_cols = softmax(st_x, cm_x, p_x, al_x, m_cols, cols)
            accumulate(n_full + d, p_x, al_x, cols)
        o_ref[0, cols, :] = (acc_ref[:dv, cols] / acc_ref[dv:dv + 1, cols]).T.astype(o_ref.dtype)


def _flash_attention(qt, k, vt, *, tq, tk):
    b, h, dqk, s = qt.shape
    nvb, vt_rows, tv = vt.shape[2:]
    dv = V_D
    assert tk % tv == 0 and s % tq == 0 and tq % (2 * tk) == 0 and vt_rows == VT_ROWS
    vmem = [2 * _nbytes((dqk, tq), BF16), 2 * _nbytes((s, dqk), BF16), 2 * _nbytes((nvb, vt_rows, tv), BF16),
            2 * _nbytes((tq, dv), BF16), 3 * _nbytes((vt_rows, tq), F32),
            2 * _nbytes((tk, tq), F32), 2 * _nbytes((tk, tq), BF16),
            3 * _nbytes((tk, tq), F32), 4 * 1024 * 1024]
    return pl.pallas_call(
        functools.partial(_flash_kernel, tq=tq, tk=tk),
        grid=(b, h, s // tq),
        in_specs=[pl.BlockSpec((1, 1, dqk, tq), lambda bb, hh, qi: (bb, hh, 0, qi)),
                  pl.BlockSpec((1, 1, s, dqk), lambda bb, hh, qi: (bb, hh, 0, 0)),
                  pl.BlockSpec((1, 1, nvb, vt_rows, tv), lambda bb, hh, qi: (bb, hh, 0, 0, 0))],
        out_specs=pl.BlockSpec((1, tq, dv), lambda bb, hh, qi: (bb, qi, hh)),
        out_shape=jax.ShapeDtypeStruct((b, s, h * dv), BF16),
        scratch_shapes=[pltpu.VMEM((vt_rows, tq), F32),
                        pltpu.VMEM((tk, tq), F32), pltpu.VMEM((tk, tq), F32),
                        pltpu.VMEM((tk, tq), BF16), pltpu.VMEM((tk, tq), BF16),
                        pltpu.VMEM((1, tq), F32), pltpu.VMEM((1, tq), F32),
                        pltpu.VMEM((1, tq), F32), pltpu.VMEM((1, tq), F32)],
        compiler_params=pltpu.CompilerParams(
            dimension_semantics=("arbitrary", "arbitrary", "arbitrary"),
            vmem_limit_bytes=_vmem_limit(*vmem)),
        name="mla_flash",
    )(qt, k, vt)


def kernel(x, positions, norm1_w, w_in, hgrn_lb, hgrn_out_norm_w, mla_q_norm_w, w_uq, mla_kv_norm_w,
           w_ukv, q_head_norm_w, k_head_norm_w, w_o, norm2_w, w_up, w_down):
    batch, seq, d = x.shape
    t = batch * seq
    depth = norm1_w.shape[0]
    hg_w = 2 * HG_HEADS * HG_DK + 2 * HG_HEADS * HG_DV
    h = x.reshape(t, d)
    for l in range(depth):
        w_in_l = w_in[l].astype(BF16)
        w_mla = jnp.concatenate([w_in_l[:, hg_w:hg_w + Q_RANK + KV_RANK],
                                 _pad_rope_cols(w_in_l[:, hg_w + Q_RANK + KV_RANK:])], axis=-1)
        n1, proj_mla = _rmsnorm_proj(h, norm1_w[l], w_mla, tm=256)
        proj_hg = _matmul([n1], [w_in_l], n_out=hg_w, out_dtype=F32, tm=1024, tn=1024, name="in_proj_hg")

        assert depth == 1
        o_hg = _hgrn2(proj_hg, hgrn_lb, hgrn_out_norm_w[l], batch=batch, seq=seq, ct=512,
                      heads_per_step=8)

        qt, k, vt = _mla_prep(proj_mla, positions, mla_q_norm_w[l], w_uq[l], mla_kv_norm_w[l], w_ukv[l],
                              q_head_norm_w[l], k_head_norm_w[l], batch=batch, seq=seq, tm=256)
        o_mla = _flash_attention(qt, k, vt, tq=1024, tk=512).reshape(t, MLA_HEADS * V_D)

        w_o_l = w_o[l].astype(BF16)
        h, h_gained, h_ssq = _matmul([o_hg, o_mla], [w_o_l, w_o_l], rhs_row_blocks=[0, 1], residual=h,
                                     next_norm_gain=norm2_w[l], out_dtype=F32, tm=1024, tn=512,
                                     name="out_proj")
        hid = _matmul([h_gained], [w_up[l].astype(BF16)], lhs_row_ssq=h_ssq, act="relu2", out_dtype=BF16,
                      tm=1024, tn=1024, name="mlp_up")
        h = _matmul([hid], [w_down[l].astype(BF16)], residual=h, out_dtype=F32, tm=1024, tn=1024,
                    tk=4096, name="mlp_down")
    return h.reshape(batch, seq, d)
```

```python
import functools
import math

import jax
import jax.numpy as jnp
from jax import lax
from jax.experimental import pallas as pl
from jax.experimental.pallas import tpu as pltpu

F32 = jnp.float32
BF16 = jnp.bfloat16

HG_HEADS = 16
HG_DK = 128
HG_DV = 128
HG_CHUNK = 64
MLA_HEADS = 16
Q_RANK = 768
KV_RANK = 512
NOPE_D = 128
ROPE_D = 64
QK_D = NOPE_D + ROPE_D
V_D = 128
ROPE_THETA = 10000.0
EPS = 1e-6

V7X_LANES = 128
V7X_BF16_SUBLANES = 16
V7X_VMEM_LIMIT_CAP = 56 * 1024 * 1024

VT_ROWS = V_D + V7X_BF16_SUBLANES

QK_PAD = 2 * V7X_LANES
ROPE_HALF = ROPE_D // 2


def _vmem_limit(*byte_counts):
    return int(min(sum(byte_counts), V7X_VMEM_LIMIT_CAP))


def _nbytes(shape, dtype):
    return math.prod(shape) * jnp.dtype(dtype).itemsize


def _rmsnorm_proj_kernel(x_ref, g_ref, w_ref, n_ref, o_ref):
    x = x_ref[...]
    ms = jnp.mean(x * x, axis=-1, keepdims=True)
    n = (x * lax.rsqrt(ms + EPS) * g_ref[...]).astype(n_ref.dtype)
    n_ref[...] = n
    o_ref[...] = jnp.dot(n, w_ref[...], preferred_element_type=F32)


def _rmsnorm_proj(x, gain, w, *, tm):
    t, d = x.shape
    n = w.shape[1]
    return pl.pallas_call(
        _rmsnorm_proj_kernel,
        grid=(t // tm,),
        in_specs=[pl.BlockSpec((tm, d), lambda i: (i, 0)),
                  pl.BlockSpec((1, d), lambda i: (0, 0)),
                  pl.BlockSpec((d, n), lambda i: (0, 0))],
        out_specs=[pl.BlockSpec((tm, d), lambda i: (i, 0)),
                   pl.BlockSpec((tm, n), lambda i: (i, 0))],
        out_shape=[jax.ShapeDtypeStruct((t, d), BF16), jax.ShapeDtypeStruct((t, n), F32)],
        compiler_params=pltpu.CompilerParams(
            dimension_semantics=("arbitrary",),
            vmem_limit_bytes=_vmem_limit(4 * _nbytes((tm, d), F32), 2 * _nbytes((tm, d), BF16),
                                         2 * _nbytes((d, n), BF16), 3 * _nbytes((tm, n), F32),
                                         4 * 1024 * 1024)),
        name="norm_in_proj_mla",
    )(x, gain.reshape(1, d).astype(F32), w)


def _matmul_kernel(*refs, n_pairs, has_res, has_gain, ssq_dim, act, nk):
    refs = list(refs)
    lhs = [refs.pop(0) for _ in range(n_pairs)]
    rhs = [refs.pop(0) for _ in range(n_pairs)]
    res_ref = refs.pop(0) if has_res else None
    gain_ref = refs.pop(0) if has_gain else None
    ssq_in_ref = refs.pop(0) if ssq_dim else None
    o_ref = refs.pop(0)
    scaled_ref, ssq_out_ref = (refs.pop(0), refs.pop(0)) if has_gain else (None, None)
    acc_ref = refs.pop(0) if refs else None

    def product():
        part = jnp.dot(lhs[0][...], rhs[0][...], preferred_element_type=F32)
        for a, b in zip(lhs[1:], rhs[1:]):
            part = part + jnp.dot(a[...], b[...], preferred_element_type=F32)
        return part

    def epilogue(v):
        if ssq_dim:
            v = v * lax.rsqrt(ssq_in_ref[:, :1] * (1.0 / ssq_dim) + EPS)
        if act == "relu2":
            v = jnp.square(jnp.maximum(v, 0.0))
        if has_res:
            v = v + res_ref[...]
        o_ref[...] = v.astype(o_ref.dtype)
        if has_gain:
            scaled_ref[...] = (v * gain_ref[...]).astype(scaled_ref.dtype)
            row_ss = jnp.broadcast_to(jnp.sum(v * v, axis=-1, keepdims=True), ssq_out_ref.shape)
            j = pl.program_id(1)

            @pl.when(j == 0)
            def _():
                ssq_out_ref[...] = row_ss

            @pl.when(j > 0)
            def _():
                ssq_out_ref[...] += row_ss

    if nk == 1:
        epilogue(product())
        return

    k = pl.program_id(2)

    if acc_ref is None:
        @pl.when(k == 0)
        def _():
            epilogue(product())

        @pl.when(k > 0)
        def _():
            o_ref[...] += product()

        return

    @pl.when(k == 0)
    def _():
        acc_ref[...] = product()

    @pl.when(jnp.logical_and(k > 0, k < nk - 1))
    def _():
        acc_ref[...] += product()

    @pl.when(k == nk - 1)
    def _():
        epilogue(acc_ref[...] + product())


def _matmul(lhs_list, rhs_list, *, residual=None, act=None, next_norm_gain=None, lhs_row_ssq=None,
            n_out=None, rhs_row_blocks=None, out_dtype, tm, tn, tk=None, name):
    m, kdim = lhs_list[0].shape
    n = rhs_list[0].shape[1] if n_out is None else n_out
    tk = kdim if tk is None else tk
    nk = kdim // tk
    assert m % tm == 0 and n % tn == 0 and kdim % tk == 0
    n_pairs = len(lhs_list)
    offsets = [0] * n_pairs if rhs_row_blocks is None else rhs_row_blocks
    in_specs = ([pl.BlockSpec((tm, tk), lambda i, j, k: (i, k))] * n_pairs
                + [pl.BlockSpec((tk, tn), functools.partial(lambda i, j, k, off: (k + off, j), off=off))
                   for off in offsets])
    args = list(lhs_list) + list(rhs_list)
    vmem = [2 * n_pairs * (_nbytes((tm, tk), BF16) + _nbytes((tk, tn), BF16)),
            2 * _nbytes((tm, tn), out_dtype),
            2 * _nbytes((tm, tn), F32)]
    tile_spec = pl.BlockSpec((tm, tn), lambda i, j, k: (i, j))
    row_spec = pl.BlockSpec((tm, V7X_LANES), lambda i, j, k: (i, 0))
    out_specs = [tile_spec]
    out_shape = [jax.ShapeDtypeStruct((m, n), out_dtype)]
    if residual is not None:
        in_specs.append(tile_spec)
        args.append(residual)
        vmem.append(2 * _nbytes((tm, tn), residual.dtype))
    if next_norm_gain is not None:
        assert nk == 1
        in_specs.append(pl.BlockSpec((1, tn), lambda i, j, k: (0, j)))
        args.append(next_norm_gain.reshape(1, n).astype(F32))
        out_specs += [tile_spec, row_spec]
        out_shape += [jax.ShapeDtypeStruct((m, n), BF16), jax.ShapeDtypeStruct((m, V7X_LANES), F32)]
        vmem.append(2 * _nbytes((tm, tn), BF16) + 2 * _nbytes((tm, tn), F32)
                    + 2 * _nbytes((tm, V7X_LANES), F32))
    if lhs_row_ssq is not None:
        in_specs.append(row_spec)
        args.append(lhs_row_ssq)
        vmem.append(2 * _nbytes((tm, V7X_LANES), F32))
    scratch = []
    plain_f32_out = (out_dtype == F32 and act is None and next_norm_gain is None
                     and lhs_row_ssq is None)
    if nk > 1 and not plain_f32_out:
        scratch.append(pltpu.VMEM((tm, tn), F32))
        vmem.append(_nbytes((tm, tn), F32))
    outs = pl.pallas_call(
        functools.partial(_matmul_kernel, n_pairs=n_pairs, has_res=residual is not None,
                          has_gain=next_norm_gain is not None,
                          ssq_dim=kdim if lhs_row_ssq is not None else 0, act=act, nk=nk),
        grid=(m // tm, n // tn, nk),
        in_specs=in_specs,
        out_specs=out_specs,
        out_shape=out_shape,
        scratch_shapes=scratch,
        compiler_params=pltpu.CompilerParams(
            dimension_semantics=("arbitrary", "arbitrary", "arbitrary"),
            vmem_limit_bytes=_vmem_limit(*vmem)),
        name=name,
    )(*args)
    return outs if next_norm_gain is not None else outs[0]


def _hgrn2_kernel(q_ref, f_ref, i_ref, g_ref, lb_ref, nw_ref, o_ref, state_ref, *, n_chunks, n_heads):
    c = HG_CHUNK

    @pl.when(pl.program_id(2) == 0)
    def _():
        state_ref[...] = jnp.zeros_like(state_ref)

    lb_raw = lb_ref[...]
    e = jnp.exp(lb_raw - jnp.max(lb_raw, axis=0, keepdims=True))
    lb_all = e[0:1, :] / jnp.sum(e, axis=0, keepdims=True)
    nw = nw_ref[...]

    row = lax.broadcasted_iota(jnp.int32, (c, c), 0)
    col = lax.broadcasted_iota(jnp.int32, (c, c), 1)
    causal = row >= col
    tri = jnp.where(causal, 1.0, 0.0).astype(BF16)

    nt_dims = (((1,), (1,)), ((), ()))
    tn_dims = (((0,), (0,)), ((), ()))
    head_cols = [slice(hd * HG_DK, (hd + 1) * HG_DK) for hd in range(n_heads)]

    def chunk(ci, carry):
        sl = pl.ds(pl.multiple_of(ci * c, c), c)
        q_raw = q_ref[sl, :]
        f_raw = f_ref[sl, :]
        q = q_raw / (1.0 + jnp.exp(-q_raw))
        f = lb_all + (1.0 - lb_all) / (1.0 + jnp.exp(-f_raw))
        k = 1.0 - f
        logf = jnp.log(f)
        hi = logf.astype(BF16)
        lo = (logf - hi.astype(F32)).astype(BF16)
        b = (jnp.dot(tri, hi, preferred_element_type=F32)
             + jnp.dot(tri, lo, preferred_element_type=F32))
        b_last = b[c - 1:c, :]
        q_dec = (q * jnp.exp(b)).astype(BF16)
        decay = jnp.exp(b_last)
        k_inv_f32 = k * jnp.exp(-b)
        k_inv = k_inv_f32.astype(BF16)
        k_tail = (k_inv_f32 * decay).astype(BF16)
        vb = i_ref[sl, :].astype(BF16)

        a = [lax.dot_general(q_dec[:, s], k_inv[:, s], nt_dims, preferred_element_type=F32)
             for s in head_cols]
        a = [jnp.where(causal, x, 0.0).astype(BF16) for x in a]
        states = [state_ref[hd] for hd in range(n_heads)]
        o = [jnp.dot(a[hd], vb[:, s], preferred_element_type=F32)
             + lax.dot_general(q_dec[:, s], states[hd].astype(BF16), nt_dims, preferred_element_type=F32)
             for hd, s in enumerate(head_cols)]
        u = [lax.dot_general(vb[:, s], k_tail[:, s], tn_dims, preferred_element_type=F32)
             for s in head_cols]
        for hd, s in enumerate(head_cols):
            state_ref[hd] = states[hd] * decay[:, s] + u[hd]

        y = [x * lax.rsqrt(jnp.mean(x * x, axis=-1, keepdims=True) + EPS) * nw for x in o]
        g_raw = g_ref[sl, :]
        gate = g_raw / (1.0 + jnp.exp(-g_raw))
        o_ref[sl, :] = (jnp.concatenate(y, axis=1) * gate).astype(o_ref.dtype)
        return carry

    lax.fori_loop(0, n_chunks, chunk, 0, unroll=4)


def _hgrn2(proj_hg, hgrn_lb, out_norm_w, *, batch, seq, ct, heads_per_step):
    t = batch * seq
    hg = heads_per_step
    n_groups = HG_HEADS // hg
    nt = seq // ct
    assert seq % ct == 0 and ct % HG_CHUNK == 0 and HG_HEADS % hg == 0

    def col_spec(section):
        return pl.BlockSpec((ct, hg * HG_DK), lambda b, gg, tt: (b * nt + tt, section * n_groups + gg))

    blk = _nbytes((ct, hg * HG_DK), F32)
    return pl.pallas_call(
        functools.partial(_hgrn2_kernel, n_chunks=ct // HG_CHUNK, n_heads=hg),
        grid=(batch, n_groups, nt),
        in_specs=[col_spec(0), col_spec(1), col_spec(2), col_spec(3),
                  pl.BlockSpec((hgrn_lb.shape[0], hg * HG_DK), lambda b, gg, tt: (0, gg)),
                  pl.BlockSpec((1, HG_DV), lambda b, gg, tt: (0, 0))],
        out_specs=pl.BlockSpec((ct, hg * HG_DV), lambda b, gg, tt: (b * nt + tt, gg)),
        out_shape=jax.ShapeDtypeStruct((t, HG_HEADS * HG_DV), BF16),
        scratch_shapes=[pltpu.VMEM((hg, HG_DV, HG_DK), F32)],
        compiler_params=pltpu.CompilerParams(
            dimension_semantics=("arbitrary", "arbitrary", "arbitrary"),
            vmem_limit_bytes=_vmem_limit(8 * blk, 2 * _nbytes((ct, hg * HG_DV), BF16),
                                         8 * 1024 * 1024)),
        name="hgrn2",
    )(proj_hg, proj_hg, proj_hg, proj_hg, hgrn_lb.astype(F32), out_norm_w.reshape(1, HG_DV).astype(F32))


def _mla_prep_kernel(p_ref, pos_ref, qnw_ref, wuq_ref, kvnw_ref, wukv_ref, qhw_ref, khw_ref,
                     invf_ref, sgn_ref, qt_out, k_out, vt_out, *, scale):
    p = p_ref[...]
    c_q = p[:, :Q_RANK]
    c_kv = p[:, Q_RANK:Q_RANK + KV_RANK]
    k_pe = p[:, Q_RANK + KV_RANK:]

    def rms(x, w):
        return x * lax.rsqrt(jnp.mean(x * x, axis=-1, keepdims=True) + EPS) * w

    q_all = jnp.dot(rms(c_q, qnw_ref[...]).astype(BF16), wuq_ref[...], preferred_element_type=F32)
    kv_all = jnp.dot(rms(c_kv, kvnw_ref[...]).astype(BF16), wukv_ref[...], preferred_element_type=F32)

    ang = pos_ref[...].astype(F32) * invf_ref[...]
    cos = jnp.cos(ang)
    sin = jnp.sin(ang) * sgn_ref[...]

    def rope(x):
        return x * cos + pltpu.roll(x, V7X_LANES // 2, 1) * sin

    qhw = qhw_ref[...]
    khw = khw_ref[...]
    inv_d = 1.0 / QK_D
    k_pe_ss = jnp.sum(k_pe * k_pe, axis=-1, keepdims=True)
    k_pe_rot = rope(k_pe * khw[:, V7X_LANES:])

    for h in range(MLA_HEADS):
        qa = q_all[:, h * QK_PAD:h * QK_PAD + V7X_LANES]
        qb = q_all[:, h * QK_PAD + V7X_LANES:(h + 1) * QK_PAD]
        ss = jnp.sum(qa * qa, axis=-1, keepdims=True) + jnp.sum(qb * qb, axis=-1, keepdims=True)
        r = lax.rsqrt(ss * inv_d + EPS) * scale
        qt_out[0, h, :V7X_LANES, :] = (qa * r * qhw[:, :V7X_LANES]).T.astype(qt_out.dtype)
        qt_out[0, h, V7X_LANES:, :] = rope(qb * r * qhw[:, V7X_LANES:]).T.astype(qt_out.dtype)

        ka = kv_all[:, h * 2 * V7X_LANES:h * 2 * V7X_LANES + NOPE_D]
        vv = kv_all[:, h * 2 * V7X_LANES + NOPE_D:(h + 1) * 2 * V7X_LANES]
        ssk = jnp.sum(ka * ka, axis=-1, keepdims=True) + k_pe_ss
        rk = lax.rsqrt(ssk * inv_d + EPS)
        k_out[0, h, :, :V7X_LANES] = (ka * rk * khw[:, :V7X_LANES]).astype(k_out.dtype)
        k_out[0, h, :, V7X_LANES:] = (k_pe_rot * rk).astype(k_out.dtype)
        vt_out[0, h, 0, :V_D, :] = vv.T.astype(vt_out.dtype)
        vt_out[0, h, 0, V_D:, :] = jnp.ones((VT_ROWS - V_D, vv.shape[0]), vt_out.dtype)


def _pad_rope_cols(w):
    z = jnp.zeros(w.shape[:-1] + (ROPE_HALF,), w.dtype)
    return jnp.concatenate([w[..., :ROPE_HALF], z, w[..., ROPE_HALF:], z], axis=-1)


def _pad_head_cols(w):
    return jnp.concatenate([w[..., :NOPE_D], _pad_rope_cols(w[..., NOPE_D:])], axis=-1)


def _mla_prep(proj_mla, positions, q_norm_w, w_uq, kv_norm_w, w_ukv, q_head_norm_w, k_head_norm_w,
              *, batch, seq, tm):
    t = batch * seq
    nh = MLA_HEADS
    pw = proj_mla.shape[1]
    wuq_p = _pad_head_cols(w_uq.reshape(Q_RANK, nh, QK_D)).reshape(Q_RANK, nh * QK_PAD).astype(BF16)
    wukv = w_ukv.astype(BF16)
    qhw = _pad_head_cols(q_head_norm_w.reshape(1, QK_D)).astype(F32)
    khw = _pad_head_cols(k_head_norm_w.reshape(1, QK_D)).astype(F32)
    inv_freq = ROPE_THETA ** (-jnp.arange(0, ROPE_D, 2, dtype=F32) / ROPE_D)
    invf = _pad_rope_cols(jnp.concatenate([inv_freq, inv_freq]).reshape(1, ROPE_D))
    sgn = _pad_rope_cols(jnp.concatenate([-jnp.ones((ROPE_HALF,), F32),
                                          jnp.ones((ROPE_HALF,), F32)]).reshape(1, ROPE_D))
    nst = seq // tm
    const = lambda i: (0, 0)
    vmem = [2 * _nbytes((tm, pw), F32), 2 * _nbytes(wuq_p.shape, BF16), 2 * _nbytes(wukv.shape, BF16),
            2 * 2 * _nbytes((nh, tm, QK_PAD), BF16), 2 * _nbytes((nh, tm, VT_ROWS), BF16),
            4 * _nbytes((tm, nh * QK_PAD), F32), 4 * 1024 * 1024]
    scale = math.log2(math.e) / math.sqrt(QK_D)
    return pl.pallas_call(
        functools.partial(_mla_prep_kernel, scale=scale),
        grid=(t // tm,),
        in_specs=[pl.BlockSpec((tm, pw), lambda i: (i, 0)),
                  pl.BlockSpec((tm, 1), lambda i: (i, 0)),
                  pl.BlockSpec((1, Q_RANK), const),
                  pl.BlockSpec(wuq_p.shape, const),
                  pl.BlockSpec((1, KV_RANK), const),
                  pl.BlockSpec(wukv.shape, const),
                  pl.BlockSpec((1, QK_PAD), const),
                  pl.BlockSpec((1, QK_PAD), const),
                  pl.BlockSpec((1, V7X_LANES), const),
                  pl.BlockSpec((1, V7X_LANES), const)],
        out_specs=[pl.BlockSpec((1, nh, QK_PAD, tm), lambda i: (i // nst, 0, 0, i % nst)),
                   pl.BlockSpec((1, nh, tm, QK_PAD), lambda i: (i // nst, 0, i % nst, 0)),
                   pl.BlockSpec((1, nh, 1, VT_ROWS, tm), lambda i: (i // nst, 0, i % nst, 0, 0))],
        out_shape=[jax.ShapeDtypeStruct((batch, nh, QK_PAD, seq), BF16),
                   jax.ShapeDtypeStruct((batch, nh, seq, QK_PAD), BF16),
                   jax.ShapeDtypeStruct((batch, nh, nst, VT_ROWS, tm), BF16)],
        compiler_params=pltpu.CompilerParams(
            dimension_semantics=("arbitrary",),
            vmem_limit_bytes=_vmem_limit(*vmem)),
        name="mla_prep",
    )(proj_mla, positions.reshape(t, 1), q_norm_w.reshape(1, Q_RANK).astype(F32), wuq_p,
      kv_norm_w.reshape(1, KV_RANK).astype(F32), wukv, qhw, khw, invf, sgn)


def _flash_kernel(qt_ref, k_ref, vt_ref, *rest, tq, tk, n_cast):
    cast_in, rest = rest[:n_cast], rest[n_cast:]
    o_ref, cast_out, rest = rest[0], rest[1:1 + n_cast], rest[1 + n_cast:]
    acc_ref, st_a, st_b, p_a, p_b, al_a, al_b, cm_a, cm_b = rest

    qi = pl.program_id(2)
    n_pairs = (tq // tk // 2) * qi
    n_full = 2 * n_pairs
    vt_tile = vt_ref.shape[-1]
    n_vt = tk // vt_tile

    def scores(blk, st_ref, cm_ref, cols, masked):
        kj = k_ref[0, 0, pl.ds(pl.multiple_of(blk * tk, tk), tk), :]
        st = jnp.dot(kj, qt_ref[0, 0, :, cols], preferred_element_type=F32)
        if masked:
            st = causal_mask(blk, st, cols)
        st_ref[:, cols] = st
        cm_ref[:, cols] = jnp.max(st, axis=0, keepdims=True)

    def causal_mask(blk, st, cols):
        n_cols = cols.stop - cols.start
        k_idx = blk * tk + lax.broadcasted_iota(jnp.int32, (tk, n_cols), 0)
        q_idx = qi * tq + cols.start + lax.broadcasted_iota(jnp.int32, (tk, n_cols), 1)
        return jnp.where(k_idx <= q_idx, st, -jnp.inf)

    def remask(blk, st_ref, cm_ref, cols):
        st = causal_mask(blk, st_ref[:, cols], cols)
        st_ref[:, cols] = st
        cm_ref[:, cols] = jnp.max(st, axis=0, keepdims=True)

    def softmax(st_ref, cm_ref, p_ref, al_ref, m_prev, cols):
        m_new = jnp.maximum(m_prev, cm_ref[:, cols])
        p_ref[:, cols] = jnp.exp2(st_ref[:, cols] - m_new).astype(BF16)
        al_ref[:, cols] = jnp.exp2(m_prev - m_new)
        return m_new

    def accumulate(blk, p_ref, al_ref, cols):
        vt = jnp.concatenate([vt_ref[0, 0, blk * n_vt + t] for t in range(n_vt)], axis=1)
        acc_ref[:, cols] = al_ref[:, cols] * acc_ref[:, cols] + jnp.dot(
            vt, p_ref[:, cols], preferred_element_type=F32)

    n_chunk = tq // (2 * V7X_LANES)
    chunks = [slice(c * 2 * V7X_LANES, (c + 1) * 2 * V7X_LANES) for c in range(n_chunk)]
    for cols in chunks:
        scores(0, st_a, cm_a, cols, False)

    for src, dst in zip(cast_in, cast_out):
        dst[...] = src[...].astype(dst.dtype)

    acc_ref[...] = jnp.zeros_like(acc_ref)
    p_b[...] = jnp.zeros_like(p_b)
    al_b[...] = jnp.ones_like(al_b)
    m_run = jnp.full((1, tq), -jnp.inf, F32)

    def pipelined(blk_next, st_next, cm_next, st_cur, cm_cur, p_cur, al_cur, blk_prev, p_prev, al_prev, m):
        m_parts = []
        for cols in chunks:
            scores(blk_next, st_next, cm_next, cols, False)
            accumulate(blk_prev, p_prev, al_prev, cols)
            m_parts.append(softmax(st_cur, cm_cur, p_cur, al_cur, m[:, cols], cols))
        return jnp.concatenate(m_parts, axis=1)

    def body(i, m):
        h = 2 * i
        m = pipelined(h + 1, st_b, cm_b, st_a, cm_a, p_a, al_a, jnp.maximum(h - 1, 0), p_b, al_b, m)
        return pipelined(h + 2, st_a, cm_a, st_b, cm_b, p_b, al_b, h, p_a, al_a, m)

    m_run = lax.fori_loop(0, n_pairs, body, m_run)
    dv = o_ref.shape[-1]
    sets = ((st_a, cm_a, p_a, al_a), (st_b, cm_b, p_b, al_b))
    for cols in chunks:
        own_block = cols.start // tk
        if own_block >= 1:
            scores(n_full + 1, st_b, cm_b, cols, own_block == 1)
        else:
            remask(n_full, st_a, cm_a, cols)
        accumulate(jnp.maximum(n_full - 1, 0), p_b, al_b, cols)
        m_cols = m_run[:, cols]
        for d in range(own_block + 1):
            st_x, cm_x, p_x, al_x = sets[d % 2]
            if d >= 2:
                scores(n_full + d, st_x, cm_x, cols, d == own_block)
            m_cols = softmax(st_x, cm_x, p_x, al_x, m_cols, cols)
            accumulate(n_full + d, p_x, al_x, cols)
        o_ref[0, cols, :] = (acc_ref[:dv, cols] / acc_ref[dv:dv + 1, cols]).T.astype(o_ref.dtype)


def _flash_attention(qt, k, vt, cast_weights, *, tq, tk):
    b, h, dqk, s = qt.shape
    nvb, vt_rows, tv = vt.shape[2:]
    dv = V_D
    nq = s // tq
    n_steps = b * h * nq
    assert tk % tv == 0 and s % tq == 0 and tq % (2 * tk) == 0 and vt_rows == VT_ROWS
    vmem = [2 * _nbytes((dqk, tq), BF16), 2 * _nbytes((s, dqk), BF16), 2 * _nbytes((nvb, vt_rows, tv), BF16),
            2 * _nbytes((tq, dv), BF16), 3 * _nbytes((vt_rows, tq), F32),
            2 * _nbytes((tk, tq), F32), 2 * _nbytes((tk, tq), BF16),
            3 * _nbytes((tk, tq), F32), 4 * 1024 * 1024]
    step_slab = lambda bb, hh, qi: ((bb * h + hh) * nq + qi, 0)
    cast_specs, cast_shapes = [], []
    for w in cast_weights:
        rows = w.shape[0] // n_steps
        assert w.shape[0] % n_steps == 0 and rows % V7X_BF16_SUBLANES == 0
        cast_specs.append(pl.BlockSpec((rows, w.shape[1]), step_slab))
        cast_shapes.append(jax.ShapeDtypeStruct(w.shape, BF16))
        vmem.append(3 * _nbytes((rows, w.shape[1]), F32) + 2 * _nbytes((rows, w.shape[1]), BF16))
    outs = pl.pallas_call(
        functools.partial(_flash_kernel, tq=tq, tk=tk, n_cast=len(cast_weights)),
        grid=(b, h, nq),
        in_specs=[pl.BlockSpec((1, 1, dqk, tq), lambda bb, hh, qi: (bb, hh, 0, qi)),
                  pl.BlockSpec((1, 1, s, dqk), lambda bb, hh, qi: (bb, hh, 0, 0)),
                  pl.BlockSpec((1, 1, nvb, vt_rows, tv), lambda bb, hh, qi: (bb, hh, 0, 0, 0))] + cast_specs,
        out_specs=[pl.BlockSpec((1, tq, dv), lambda bb, hh, qi: (bb, qi, hh))] + cast_specs,
        out_shape=[jax.ShapeDtypeStruct((b, s, h * dv), BF16)] + cast_shapes,
        scratch_shapes=[pltpu.VMEM((vt_rows, tq), F32),
                        pltpu.VMEM((tk, tq), F32), pltpu.VMEM((tk, tq), F32),
                        pltpu.VMEM((tk, tq), BF16), pltpu.VMEM((tk, tq), BF16),
                        pltpu.VMEM((1, tq), F32), pltpu.VMEM((1, tq), F32),
                        pltpu.VMEM((1, tq), F32), pltpu.VMEM((1, tq), F32)],
        compiler_params=pltpu.CompilerParams(
            dimension_semantics=("arbitrary", "arbitrary", "arbitrary"),
            vmem_limit_bytes=_vmem_limit(*vmem)),
        name="mla_flash",
    )(qt, k, vt, *cast_weights)
    return outs[0], outs[1:]


def kernel(x, positions, norm1_w, w_in, hgrn_lb, hgrn_out_norm_w, mla_q_norm_w, w_uq, mla_kv_norm_w,
           w_ukv, q_head_norm_w, k_head_norm_w, w_o, norm2_w, w_up, w_down):
    batch, seq, d = x.shape
    t = batch * seq
    depth = norm1_w.shape[0]
    hg_w = 2 * HG_HEADS * HG_DK + 2 * HG_HEADS * HG_DV
    h = x.reshape(t, d)
    for l in range(depth):
        w_in_l = w_in[l].astype(BF16)
        w_mla = jnp.concatenate([w_in_l[:, hg_w:hg_w + Q_RANK + KV_RANK],
                                 _pad_rope_cols(w_in_l[:, hg_w + Q_RANK + KV_RANK:])], axis=-1)
        n1, proj_mla = _rmsnorm_proj(h, norm1_w[l], w_mla, tm=256)
        proj_hg = _matmul([n1], [w_in_l], n_out=hg_w, out_dtype=F32, tm=1024, tn=1024, name="in_proj_hg")

        assert depth == 1
        o_hg = _hgrn2(proj_hg, hgrn_lb, hgrn_out_norm_w[l], batch=batch, seq=seq, ct=512,
                      heads_per_step=8)

        qt, k, vt = _mla_prep(proj_mla, positions, mla_q_norm_w[l], w_uq[l], mla_kv_norm_w[l], w_ukv[l],
                              q_head_norm_w[l], k_head_norm_w[l], batch=batch, seq=seq, tm=256)
        o_mla, (w_o_l, w_up_l, w_down_l) = _flash_attention(qt, k, vt, [w_o[l], w_up[l], w_down[l]],
                                                            tq=1024, tk=512)
        o_mla = o_mla.reshape(t, MLA_HEADS * V_D)

        h, h_gained, h_ssq = _matmul([o_hg, o_mla], [w_o_l, w_o_l], rhs_row_blocks=[0, 1], residual=h,
                                     next_norm_gain=norm2_w[l], out_dtype=F32, tm=1024, tn=512,
                                     name="out_proj")
        hid = _matmul([h_gained], [w_up_l], lhs_row_ssq=h_ssq, act="relu2", out_dtype=BF16,
                      tm=1024, tn=1024, name="mlp_up")
        h = _matmul([hid], [w_down_l], residual=h, out_dtype=F32, tm=1024, tn=1024,
                    tk=4096, name="mlp_down")
    return h.reshape(batch, seq, d)
```

```python
import functools
import math

import jax
import jax.numpy as jnp
from jax import lax
from jax.experimental import pallas as pl
from jax.experimental.pallas import tpu as pltpu

F32 = jnp.float32
BF16 = jnp.bfloat16

HG_HEADS = 16
HG_DK = 128
HG_DV = 128
HG_CHUNK = 64
MLA_HEADS = 16
Q_RANK = 768
KV_RANK = 512
NOPE_D = 128
ROPE_D = 64
QK_D = NOPE_D + ROPE_D
V_D = 128
ROPE_THETA = 10000.0
EPS = 1e-6

V7X_LANES = 128
V7X_BF16_SUBLANES = 16
V7X_VMEM_LIMIT_CAP = 56 * 1024 * 1024

VT_ROWS = V_D + V7X_BF16_SUBLANES

QK_PAD = 2 * V7X_LANES
ROPE_HALF = ROPE_D // 2


def _vmem_limit(*byte_counts):
    return int(min(sum(byte_counts), V7X_VMEM_LIMIT_CAP))


def _nbytes(shape, dtype):
    return math.prod(shape) * jnp.dtype(dtype).itemsize


def _rmsnorm_proj_kernel(x_ref, g_ref, w_ref, n_ref, o_ref):
    x = x_ref[...]
    ms = jnp.mean(x * x, axis=-1, keepdims=True)
    n = (x * lax.rsqrt(ms + EPS) * g_ref[...]).astype(n_ref.dtype)
    n_ref[...] = n
    o_ref[...] = jnp.dot(n, w_ref[...], preferred_element_type=F32)


def _rmsnorm_proj(x, gain, w, *, tm):
    t, d = x.shape
    n = w.shape[1]
    return pl.pallas_call(
        _rmsnorm_proj_kernel,
        grid=(t // tm,),
        in_specs=[pl.BlockSpec((tm, d), lambda i: (i, 0)),
                  pl.BlockSpec((1, d), lambda i: (0, 0)),
                  pl.BlockSpec((d, n), lambda i: (0, 0))],
        out_specs=[pl.BlockSpec((tm, d), lambda i: (i, 0)),
                   pl.BlockSpec((tm, n), lambda i: (i, 0))],
        out_shape=[jax.ShapeDtypeStruct((t, d), BF16), jax.ShapeDtypeStruct((t, n), F32)],
        compiler_params=pltpu.CompilerParams(
            dimension_semantics=("arbitrary",),
            vmem_limit_bytes=_vmem_limit(4 * _nbytes((tm, d), F32), 2 * _nbytes((tm, d), BF16),
                                         2 * _nbytes((d, n), BF16), 3 * _nbytes((tm, n), F32),
                                         4 * 1024 * 1024)),
        name="norm_in_proj_mla",
    )(x, gain.reshape(1, d).astype(F32), w)


def _matmul_kernel(*refs, n_pairs, has_res, has_gain, ssq_dim, act, nk, rhs_transposed):
    refs = list(refs)
    lhs = [refs.pop(0) for _ in range(n_pairs)]
    rhs = [refs.pop(0) for _ in range(n_pairs)]
    res_ref = refs.pop(0) if has_res else None
    gain_ref = refs.pop(0) if has_gain else None
    ssq_in_ref = refs.pop(0) if ssq_dim else None
    o_ref = refs.pop(0)
    scaled_ref, ssq_out_ref = (refs.pop(0), refs.pop(0)) if has_gain else (None, None)
    acc_ref = refs.pop(0) if refs else None
    dims = (((1,), (1 if rhs_transposed else 0,)), ((), ()))

    def product():
        part = lax.dot_general(lhs[0][...], rhs[0][...], dims, preferred_element_type=F32)
        for a, b in zip(lhs[1:], rhs[1:]):
            part = part + lax.dot_general(a[...], b[...], dims, preferred_element_type=F32)
        return part

    def epilogue(v):
        if ssq_dim:
            v = v * lax.rsqrt(ssq_in_ref[:, :1] * (1.0 / ssq_dim) + EPS)
        if act == "relu2":
            v = jnp.square(jnp.maximum(v, 0.0))
        if has_res:
            v = v + res_ref[...]
        o_ref[...] = v.astype(o_ref.dtype)
        if has_gain:
            scaled_ref[...] = (v * gain_ref[...]).astype(scaled_ref.dtype)
            row_ss = jnp.broadcast_to(jnp.sum(v * v, axis=-1, keepdims=True), ssq_out_ref.shape)
            j = pl.program_id(1)

            @pl.when(j == 0)
            def _():
                ssq_out_ref[...] = row_ss

            @pl.when(j > 0)
            def _():
                ssq_out_ref[...] += row_ss

    if nk == 1:
        epilogue(product())
        return

    k = pl.program_id(2)

    if acc_ref is None:
        @pl.when(k == 0)
        def _():
            epilogue(product())

        @pl.when(k > 0)
        def _():
            o_ref[...] += product()

        return

    @pl.when(k == 0)
    def _():
        acc_ref[...] = product()

    @pl.when(jnp.logical_and(k > 0, k < nk - 1))
    def _():
        acc_ref[...] += product()

    @pl.when(k == nk - 1)
    def _():
        epilogue(acc_ref[...] + product())


def _matmul(lhs_list, rhs_list, *, residual=None, act=None, next_norm_gain=None, lhs_row_ssq=None,
            rhs_row_blocks=None, rhs_transposed=False, out_dtype, tm, tn, tk=None, name):
    m, kdim = lhs_list[0].shape
    n = rhs_list[0].shape[0 if rhs_transposed else 1]
    tk = kdim if tk is None else tk
    nk = kdim // tk
    assert m % tm == 0 and n % tn == 0 and kdim % tk == 0
    n_pairs = len(lhs_list)
    offsets = [0] * n_pairs if rhs_row_blocks is None else rhs_row_blocks
    if rhs_transposed:
        assert rhs_row_blocks is None
        rhs_specs = [pl.BlockSpec((tn, tk), lambda i, j, k: (j, k))] * n_pairs
    else:
        rhs_specs = [pl.BlockSpec((tk, tn), functools.partial(lambda i, j, k, off: (k + off, j), off=off))
                     for off in offsets]
    in_specs = [pl.BlockSpec((tm, tk), lambda i, j, k: (i, k))] * n_pairs + rhs_specs
    args = list(lhs_list) + list(rhs_list)
    vmem = [2 * n_pairs * (_nbytes((tm, tk), BF16) + _nbytes((tk, tn), BF16)),
            2 * _nbytes((tm, tn), out_dtype),
            2 * _nbytes((tm, tn), F32)]
    tile_spec = pl.BlockSpec((tm, tn), lambda i, j, k: (i, j))
    row_spec = pl.BlockSpec((tm, V7X_LANES), lambda i, j, k: (i, 0))
    out_specs = [tile_spec]
    out_shape = [jax.ShapeDtypeStruct((m, n), out_dtype)]
    if residual is not None:
        in_specs.append(tile_spec)
        args.append(residual)
        vmem.append(2 * _nbytes((tm, tn), residual.dtype))
    if next_norm_gain is not None:
        assert nk == 1
        in_specs.append(pl.BlockSpec((1, tn), lambda i, j, k: (0, j)))
        args.append(next_norm_gain.reshape(1, n).astype(F32))
        out_specs += [tile_spec, row_spec]
        out_shape += [jax.ShapeDtypeStruct((m, n), BF16), jax.ShapeDtypeStruct((m, V7X_LANES), F32)]
        vmem.append(2 * _nbytes((tm, tn), BF16) + 2 * _nbytes((tm, tn), F32)
                    + 2 * _nbytes((tm, V7X_LANES), F32))
    if lhs_row_ssq is not None:
        in_specs.append(row_spec)
        args.append(lhs_row_ssq)
        vmem.append(2 * _nbytes((tm, V7X_LANES), F32))
    scratch = []
    plain_f32_out = (out_dtype == F32 and act is None and next_norm_gain is None
                     and lhs_row_ssq is None)
    if nk > 1 and not plain_f32_out:
        scratch.append(pltpu.VMEM((tm, tn), F32))
        vmem.append(_nbytes((tm, tn), F32))
    outs = pl.pallas_call(
        functools.partial(_matmul_kernel, n_pairs=n_pairs, has_res=residual is not None,
                          has_gain=next_norm_gain is not None,
                          ssq_dim=kdim if lhs_row_ssq is not None else 0, act=act, nk=nk,
                          rhs_transposed=rhs_transposed),
        grid=(m // tm, n // tn, nk),
        in_specs=in_specs,
        out_specs=out_specs,
        out_shape=out_shape,
        scratch_shapes=scratch,
        compiler_params=pltpu.CompilerParams(
            dimension_semantics=("arbitrary", "arbitrary", "arbitrary"),
            vmem_limit_bytes=_vmem_limit(*vmem)),
        name=name,
    )(*args)
    return outs if next_norm_gain is not None else outs[0]


def _hgrn2_kernel(q_ref, f_ref, i_ref, g_ref, lb_ref, nw_ref, o_ref, state_ref, *, n_chunks, n_heads):
    c = HG_CHUNK

    @pl.when(pl.program_id(2) == 0)
    def _():
        state_ref[...] = jnp.zeros_like(state_ref)

    lb_raw = lb_ref[...]
    e = jnp.exp(lb_raw - jnp.max(lb_raw, axis=0, keepdims=True))
    lb_all = e[0:1, :] / jnp.sum(e, axis=0, keepdims=True)
    nw = nw_ref[...]

    row = lax.broadcasted_iota(jnp.int32, (c, c), 0)
    col = lax.broadcasted_iota(jnp.int32, (c, c), 1)
    causal = row >= col
    tri = jnp.where(causal, 1.0, 0.0).astype(BF16)

    nt_dims = (((1,), (1,)), ((), ()))
    tn_dims = (((0,), (0,)), ((), ()))
    head_cols = [slice(hd * HG_DK, (hd + 1) * HG_DK) for hd in range(n_heads)]

    def chunk(ci, carry):
        sl = pl.ds(pl.multiple_of(ci * c, c), c)
        q_raw = q_ref[sl, :]
        f_raw = f_ref[sl, :]
        q = q_raw / (1.0 + jnp.exp(-q_raw))
        f = lb_all + (1.0 - lb_all) / (1.0 + jnp.exp(-f_raw))
        k = 1.0 - f
        logf = jnp.log(f)
        hi = logf.astype(BF16)
        lo = (logf - hi.astype(F32)).astype(BF16)
        b = (jnp.dot(tri, hi, preferred_element_type=F32)
             + jnp.dot(tri, lo, preferred_element_type=F32))
        b_last = b[c - 1:c, :]
        q_dec = (q * jnp.exp(b)).astype(BF16)
        decay = jnp.exp(b_last)
        k_inv_f32 = k * jnp.exp(-b)
        k_inv = k_inv_f32.astype(BF16)
        k_tail = (k_inv_f32 * decay).astype(BF16)
        vb = i_ref[sl, :].astype(BF16)

        a = [lax.dot_general(q_dec[:, s], k_inv[:, s], nt_dims, preferred_element_type=F32)
             for s in head_cols]
        a = [jnp.where(causal, x, 0.0).astype(BF16) for x in a]
        states = [state_ref[hd] for hd in range(n_heads)]
        o = [jnp.dot(a[hd], vb[:, s], preferred_element_type=F32)
             + lax.dot_general(q_dec[:, s], states[hd].astype(BF16), nt_dims, preferred_element_type=F32)
             for hd, s in enumerate(head_cols)]
        u = [lax.dot_general(vb[:, s], k_tail[:, s], tn_dims, preferred_element_type=F32)
             for s in head_cols]
        for hd, s in enumerate(head_cols):
            state_ref[hd] = states[hd] * decay[:, s] + u[hd]

        y = [x * lax.rsqrt(jnp.mean(x * x, axis=-1, keepdims=True) + EPS) * nw for x in o]
        g_raw = g_ref[sl, :]
        gate = g_raw / (1.0 + jnp.exp(-g_raw))
        o_ref[sl, :] = (jnp.concatenate(y, axis=1) * gate).astype(o_ref.dtype)
        return carry

    lax.fori_loop(0, n_chunks, chunk, 0, unroll=4)


def _hgrn2(proj_hg, hgrn_lb, out_norm_w, *, batch, seq, ct, heads_per_step):
    t = batch * seq
    hg = heads_per_step
    n_groups = HG_HEADS // hg
    nt = seq // ct
    assert seq % ct == 0 and ct % HG_CHUNK == 0 and HG_HEADS % hg == 0

    def col_spec(section):
        return pl.BlockSpec((ct, hg * HG_DK), lambda b, gg, tt: (b * nt + tt, section * n_groups + gg))

    blk = _nbytes((ct, hg * HG_DK), F32)
    return pl.pallas_call(
        functools.partial(_hgrn2_kernel, n_chunks=ct // HG_CHUNK, n_heads=hg),
        grid=(batch, n_groups, nt),
        in_specs=[col_spec(0), col_spec(1), col_spec(2), col_spec(3),
                  pl.BlockSpec((hgrn_lb.shape[0], hg * HG_DK), lambda b, gg, tt: (0, gg)),
                  pl.BlockSpec((1, HG_DV), lambda b, gg, tt: (0, 0))],
        out_specs=pl.BlockSpec((ct, hg * HG_DV), lambda b, gg, tt: (b * nt + tt, gg)),
        out_shape=jax.ShapeDtypeStruct((t, HG_HEADS * HG_DV), BF16),
        scratch_shapes=[pltpu.VMEM((hg, HG_DV, HG_DK), F32)],
        compiler_params=pltpu.CompilerParams(
            dimension_semantics=("arbitrary", "arbitrary", "arbitrary"),
            vmem_limit_bytes=_vmem_limit(8 * blk, 2 * _nbytes((ct, hg * HG_DV), BF16),
                                         8 * 1024 * 1024)),
        name="hgrn2",
    )(proj_hg, proj_hg, proj_hg, proj_hg, hgrn_lb.astype(F32), out_norm_w.reshape(1, HG_DV).astype(F32))


def _mla_prep_kernel(p_ref, pos_ref, qnw_ref, wuq_ref, kvnw_ref, wukv_ref, qhw_ref, khw_ref,
                     invf_ref, sgn_ref, qt_out, k_out, vt_out, *, scale):
    p = p_ref[...]
    c_q = p[:, :Q_RANK]
    c_kv = p[:, Q_RANK:Q_RANK + KV_RANK]
    k_pe = p[:, Q_RANK + KV_RANK:]

    def rms(x, w):
        return x * lax.rsqrt(jnp.mean(x * x, axis=-1, keepdims=True) + EPS) * w

    q_all = jnp.dot(rms(c_q, qnw_ref[...]).astype(BF16), wuq_ref[...], preferred_element_type=F32)
    kv_all = jnp.dot(rms(c_kv, kvnw_ref[...]).astype(BF16), wukv_ref[...], preferred_element_type=F32)

    ang = pos_ref[...].astype(F32) * invf_ref[...]
    cos = jnp.cos(ang)
    sin = jnp.sin(ang) * sgn_ref[...]

    def rope(x):
        return x * cos + pltpu.roll(x, V7X_LANES // 2, 1) * sin

    qhw = qhw_ref[...]
    khw = khw_ref[...]
    inv_d = 1.0 / QK_D
    k_pe_ss = jnp.sum(k_pe * k_pe, axis=-1, keepdims=True)
    k_pe_rot = rope(k_pe * khw[:, V7X_LANES:])

    for h in range(MLA_HEADS):
        qa = q_all[:, h * QK_PAD:h * QK_PAD + V7X_LANES]
        qb = q_all[:, h * QK_PAD + V7X_LANES:(h + 1) * QK_PAD]
        ss = jnp.sum(qa * qa, axis=-1, keepdims=True) + jnp.sum(qb * qb, axis=-1, keepdims=True)
        r = lax.rsqrt(ss * inv_d + EPS) * scale
        qt_out[0, h, :V7X_LANES, :] = (qa * r * qhw[:, :V7X_LANES]).T.astype(qt_out.dtype)
        qt_out[0, h, V7X_LANES:, :] = rope(qb * r * qhw[:, V7X_LANES:]).T.astype(qt_out.dtype)

        ka = kv_all[:, h * 2 * V7X_LANES:h * 2 * V7X_LANES + NOPE_D]
        vv = kv_all[:, h * 2 * V7X_LANES + NOPE_D:(h + 1) * 2 * V7X_LANES]
        ssk = jnp.sum(ka * ka, axis=-1, keepdims=True) + k_pe_ss
        rk = lax.rsqrt(ssk * inv_d + EPS)
        k_out[0, h, :, :V7X_LANES] = (ka * rk * khw[:, :V7X_LANES]).astype(k_out.dtype)
        k_out[0, h, :, V7X_LANES:] = (k_pe_rot * rk).astype(k_out.dtype)
        vt_out[0, h, 0, :V_D, :] = vv.T.astype(vt_out.dtype)
        vt_out[0, h, 0, V_D:, :] = jnp.ones((VT_ROWS - V_D, vv.shape[0]), vt_out.dtype)


def _pad_rope_cols(w):
    z = jnp.zeros(w.shape[:-1] + (ROPE_HALF,), w.dtype)
    return jnp.concatenate([w[..., :ROPE_HALF], z, w[..., ROPE_HALF:], z], axis=-1)


def _pad_head_cols(w):
    return jnp.concatenate([w[..., :NOPE_D], _pad_rope_cols(w[..., NOPE_D:])], axis=-1)


def _mla_prep(proj_mla, positions, q_norm_w, w_uq, kv_norm_w, w_ukv, q_head_norm_w, k_head_norm_w,
              *, batch, seq, tm):
    t = batch * seq
    nh = MLA_HEADS
    pw = proj_mla.shape[1]
    wuq_p = _pad_head_cols(w_uq.reshape(Q_RANK, nh, QK_D)).reshape(Q_RANK, nh * QK_PAD).astype(BF16)
    wukv = w_ukv.astype(BF16)
    qhw = _pad_head_cols(q_head_norm_w.reshape(1, QK_D)).astype(F32)
    khw = _pad_head_cols(k_head_norm_w.reshape(1, QK_D)).astype(F32)
    inv_freq = ROPE_THETA ** (-jnp.arange(0, ROPE_D, 2, dtype=F32) / ROPE_D)
    invf = _pad_rope_cols(jnp.concatenate([inv_freq, inv_freq]).reshape(1, ROPE_D))
    sgn = _pad_rope_cols(jnp.concatenate([-jnp.ones((ROPE_HALF,), F32),
                                          jnp.ones((ROPE_HALF,), F32)]).reshape(1, ROPE_D))
    nst = seq // tm
    const = lambda i: (0, 0)
    vmem = [2 * _nbytes((tm, pw), F32), 2 * _nbytes(wuq_p.shape, BF16), 2 * _nbytes(wukv.shape, BF16),
            2 * 2 * _nbytes((nh, tm, QK_PAD), BF16), 2 * _nbytes((nh, tm, VT_ROWS), BF16),
            4 * _nbytes((tm, nh * QK_PAD), F32), 4 * 1024 * 1024]
    scale = math.log2(math.e) / math.sqrt(QK_D)
    return pl.pallas_call(
        functools.partial(_mla_prep_kernel, scale=scale),
        grid=(t // tm,),
        in_specs=[pl.BlockSpec((tm, pw), lambda i: (i, 0)),
                  pl.BlockSpec((tm, 1), lambda i: (i, 0)),
                  pl.BlockSpec((1, Q_RANK), const),
                  pl.BlockSpec(wuq_p.shape, const),
                  pl.BlockSpec((1, KV_RANK), const),
                  pl.BlockSpec(wukv.shape, const),
                  pl.BlockSpec((1, QK_PAD), const),
                  pl.BlockSpec((1, QK_PAD), const),
                  pl.BlockSpec((1, V7X_LANES), const),
                  pl.BlockSpec((1, V7X_LANES), const)],
        out_specs=[pl.BlockSpec((1, nh, QK_PAD, tm), lambda i: (i // nst, 0, 0, i % nst)),
                   pl.BlockSpec((1, nh, tm, QK_PAD), lambda i: (i // nst, 0, i % nst, 0)),
                   pl.BlockSpec((1, nh, 1, VT_ROWS, tm), lambda i: (i // nst, 0, i % nst, 0, 0))],
        out_shape=[jax.ShapeDtypeStruct((batch, nh, QK_PAD, seq), BF16),
                   jax.ShapeDtypeStruct((batch, nh, seq, QK_PAD), BF16),
                   jax.ShapeDtypeStruct((batch, nh, nst, VT_ROWS, tm), BF16)],
        compiler_params=pltpu.CompilerParams(
            dimension_semantics=("arbitrary",),
            vmem_limit_bytes=_vmem_limit(*vmem)),
        name="mla_prep",
    )(proj_mla, positions.reshape(t, 1), q_norm_w.reshape(1, Q_RANK).astype(F32), wuq_p,
      kv_norm_w.reshape(1, KV_RANK).astype(F32), wukv, qhw, khw, invf, sgn)


def _flash_kernel(qt_ref, k_ref, vt_ref, *rest, tq, tk, n_cast):
    cast_in, rest = rest[:n_cast], rest[n_cast:]
    o_ref, cast_out, rest = rest[0], rest[1:1 + n_cast], rest[1 + n_cast:]
    acc_ref, st_a, st_b, p_a, p_b, al_a, al_b, cm_a, cm_b = rest

    qi = pl.program_id(2)
    n_pairs = (tq // tk // 2) * qi
    n_full = 2 * n_pairs
    vt_tile = vt_ref.shape[-1]
    n_vt = tk // vt_tile

    def scores(blk, st_ref, cm_ref, cols, masked):
        kj = k_ref[0, 0, pl.ds(pl.multiple_of(blk * tk, tk), tk), :]
        st = jnp.dot(kj, qt_ref[0, 0, :, cols], preferred_element_type=F32)
        if masked:
            st = causal_mask(blk, st, cols)
        st_ref[:, cols] = st
        cm_ref[:, cols] = jnp.max(st, axis=0, keepdims=True)

    def causal_mask(blk, st, cols):
        n_cols = cols.stop - cols.start
        k_idx = blk * tk + lax.broadcasted_iota(jnp.int32, (tk, n_cols), 0)
        q_idx = qi * tq + cols.start + lax.broadcasted_iota(jnp.int32, (tk, n_cols), 1)
        return jnp.where(k_idx <= q_idx, st, -jnp.inf)

    def remask(blk, st_ref, cm_ref, cols):
        st = causal_mask(blk, st_ref[:, cols], cols)
        st_ref[:, cols] = st
        cm_ref[:, cols] = jnp.max(st, axis=0, keepdims=True)

    def softmax(st_ref, cm_ref, p_ref, al_ref, m_prev, cols):
        m_new = jnp.maximum(m_prev, cm_ref[:, cols])
        p_ref[:, cols] = jnp.exp2(st_ref[:, cols] - m_new).astype(BF16)
        al_ref[:, cols] = jnp.exp2(m_prev - m_new)
        return m_new

    def accumulate(blk, p_ref, al_ref, cols):
        vt = jnp.concatenate([vt_ref[0, 0, blk * n_vt + t] for t in range(n_vt)], axis=1)
        acc_ref[:, cols] = al_ref[:, cols] * acc_ref[:, cols] + jnp.dot(
            vt, p_ref[:, cols], preferred_element_type=F32)

    n_chunk = tq // (2 * V7X_LANES)
    chunks = [slice(c * 2 * V7X_LANES, (c + 1) * 2 * V7X_LANES) for c in range(n_chunk)]
    for cols in chunks:
        scores(0, st_a, cm_a, cols, False)

    for src, dst in zip(cast_in, cast_out):
        dst[...] = src[...].astype(dst.dtype)

    acc_ref[...] = jnp.zeros_like(acc_ref)
    p_b[...] = jnp.zeros_like(p_b)
    al_b[...] = jnp.ones_like(al_b)
    m_run = jnp.full((1, tq), -jnp.inf, F32)

    def pipelined(blk_next, st_next, cm_next, st_cur, cm_cur, p_cur, al_cur, blk_prev, p_prev, al_prev, m):
        m_parts = []
        for cols in chunks:
            scores(blk_next, st_next, cm_next, cols, False)
            accumulate(blk_prev, p_prev, al_prev, cols)
            m_parts.append(softmax(st_cur, cm_cur, p_cur, al_cur, m[:, cols], cols))
        return jnp.concatenate(m_parts, axis=1)

    def body(i, m):
        h = 2 * i
        m = pipelined(h + 1, st_b, cm_b, st_a, cm_a, p_a, al_a, jnp.maximum(h - 1, 0), p_b, al_b, m)
        return pipelined(h + 2, st_a, cm_a, st_b, cm_b, p_b, al_b, h, p_a, al_a, m)

    m_run = lax.fori_loop(0, n_pairs, body, m_run)
    dv = o_ref.shape[-1]
    sets = ((st_a, cm_a, p_a, al_a), (st_b, cm_b, p_b, al_b))
    for cols in chunks:
        own_block = cols.start // tk
        if own_block >= 1:
            scores(n_full + 1, st_b, cm_b, cols, own_block == 1)
        else:
            remask(n_full, st_a, cm_a, cols)
        accumulate(jnp.maximum(n_full - 1, 0), p_b, al_b, cols)
        m_cols = m_run[:, cols]
        for d in range(own_block + 1):
            st_x, cm_x, p_x, al_x = sets[d % 2]
            if d >= 2:
                scores(n_full + d, st_x, cm_x, cols, d == own_block)
            m_cols = softmax(st_x, cm_x, p_x, al_x, m_cols, cols)
            accumulate(n_full + d, p_x, al_x, cols)
        o_ref[0, cols, :] = (acc_ref[:dv, cols] / acc_ref[dv:dv + 1, cols]).T.astype(o_ref.dtype)


def _flash_attention(qt, k, vt, cast_weights, *, tq, tk):
    b, h, dqk, s = qt.shape
    nvb, vt_rows, tv = vt.shape[2:]
    dv = V_D
    nq = s // tq
    n_steps = b * h * nq
    assert tk % tv == 0 and s % tq == 0 and tq % (2 * tk) == 0 and vt_rows == VT_ROWS
    vmem = [2 * _nbytes((dqk, tq), BF16), 2 * _nbytes((s, dqk), BF16), 2 * _nbytes((nvb, vt_rows, tv), BF16),
            2 * _nbytes((tq, dv), BF16), 3 * _nbytes((vt_rows, tq), F32),
            2 * _nbytes((tk, tq), F32), 2 * _nbytes((tk, tq), BF16),
            3 * _nbytes((tk, tq), F32), 4 * 1024 * 1024]
    step_slab = lambda bb, hh, qi: ((bb * h + hh) * nq + qi, 0)
    cast_specs, cast_shapes = [], []
    for w, n_rows in cast_weights:
        rows, n_cols = n_rows // n_steps, w.shape[1]
        assert n_rows % n_steps == 0 and rows % V7X_BF16_SUBLANES == 0 and n_cols % V7X_LANES == 0
        cast_specs.append(pl.BlockSpec((rows, n_cols), step_slab))
        cast_shapes.append(jax.ShapeDtypeStruct((n_rows, n_cols), BF16))
        vmem.append(3 * _nbytes((rows, n_cols), F32) + 2 * _nbytes((rows, n_cols), BF16))
    outs = pl.pallas_call(
        functools.partial(_flash_kernel, tq=tq, tk=tk, n_cast=len(cast_weights)),
        grid=(b, h, nq),
        in_specs=[pl.BlockSpec((1, 1, dqk, tq), lambda bb, hh, qi: (bb, hh, 0, qi)),
                  pl.BlockSpec((1, 1, s, dqk), lambda bb, hh, qi: (bb, hh, 0, 0)),
                  pl.BlockSpec((1, 1, nvb, vt_rows, tv), lambda bb, hh, qi: (bb, hh, 0, 0, 0))] + cast_specs,
        out_specs=[pl.BlockSpec((1, tq, dv), lambda bb, hh, qi: (bb, qi, hh))] + cast_specs,
        out_shape=[jax.ShapeDtypeStruct((b, s, h * dv), BF16)] + cast_shapes,
        scratch_shapes=[pltpu.VMEM((vt_rows, tq), F32),
                        pltpu.VMEM((tk, tq), F32), pltpu.VMEM((tk, tq), F32),
                        pltpu.VMEM((tk, tq), BF16), pltpu.VMEM((tk, tq), BF16),
                        pltpu.VMEM((1, tq), F32), pltpu.VMEM((1, tq), F32),
                        pltpu.VMEM((1, tq), F32), pltpu.VMEM((1, tq), F32)],
        compiler_params=pltpu.CompilerParams(
            dimension_semantics=("arbitrary", "arbitrary", "arbitrary"),
            vmem_limit_bytes=_vmem_limit(*vmem)),
        name="mla_flash",
    )(qt, k, vt, *[w for w, _ in cast_weights])
    return outs[0], outs[1:]


def kernel(x, positions, norm1_w, w_in, hgrn_lb, hgrn_out_norm_w, mla_q_norm_w, w_uq, mla_kv_norm_w,
           w_ukv, q_head_norm_w, k_head_norm_w, w_o, norm2_w, w_up, w_down):
    batch, seq, d = x.shape
    t = batch * seq
    depth = norm1_w.shape[0]
    hg_w = 2 * HG_HEADS * HG_DK + 2 * HG_HEADS * HG_DV
    h = x.reshape(t, d)
    for l in range(depth):
        w_in_t = jnp.swapaxes(w_in[l], 0, 1)
        w_mla_t = w_in_t[hg_w:]
        w_mla = jnp.swapaxes(jnp.concatenate(
            [w_mla_t[:Q_RANK + KV_RANK],
             jnp.swapaxes(_pad_rope_cols(jnp.swapaxes(w_mla_t[Q_RANK + KV_RANK:], 0, 1)), 0, 1)],
            axis=0), 0, 1).astype(BF16)
        n1, proj_mla = _rmsnorm_proj(h, norm1_w[l], w_mla, tm=256)
        qt, k, vt = _mla_prep(proj_mla, positions, mla_q_norm_w[l], w_uq[l], mla_kv_norm_w[l], w_ukv[l],
                              q_head_norm_w[l], k_head_norm_w[l], batch=batch, seq=seq, tm=256)
        o_mla, (w_hg_t, w_o_l, w_up_l, w_down_l) = _flash_attention(
            qt, k, vt, [(w_in_t, hg_w), (w_o[l], d), (w_up[l], d), (w_down[l], w_down.shape[1])],
            tq=1024, tk=512)
        o_mla = o_mla.reshape(t, MLA_HEADS * V_D)

        proj_hg = _matmul([n1], [w_hg_t], rhs_transposed=True, out_dtype=F32, tm=1024, tn=1024,
                          name="in_proj_hg")
        assert depth == 1
        o_hg = _hgrn2(proj_hg, hgrn_lb, hgrn_out_norm_w[l], batch=batch, seq=seq, ct=512,
                      heads_per_step=8)

        h, h_gained, h_ssq = _matmul([o_hg, o_mla], [w_o_l, w_o_l], rhs_row_blocks=[0, 1], residual=h,
                                     next_norm_gain=norm2_w[l], out_dtype=F32, tm=1024, tn=512,
                                     name="out_proj")
        hid = _matmul([h_gained], [w_up_l], lhs_row_ssq=h_ssq, act="relu2", out_dtype=BF16,
                      tm=1024, tn=1024, name="mlp_up")
        h = _matmul([hid], [w_down_l], residual=h, out_dtype=F32, tm=1024, tn=1024,
                    tk=4096, name="mlp_down")
    return h.reshape(batch, seq, d)
```

```python
import functools
import math

import jax
import jax.numpy as jnp
from jax import lax
from jax.experimental import pallas as pl
from jax.experimental.pallas import tpu as pltpu

F32 = jnp.float32
BF16 = jnp.bfloat16

HG_HEADS = 16
HG_DK = 128
HG_DV = 128
HG_CHUNK = 64
MLA_HEADS = 16
Q_RANK = 768
KV_RANK = 512
NOPE_D = 128
ROPE_D = 64
QK_D = NOPE_D + ROPE_D
V_D = 128
ROPE_THETA = 10000.0
EPS = 1e-6

V7X_LANES = 128
V7X_BF16_SUBLANES = 16
V7X_VMEM_LIMIT_CAP = 56 * 1024 * 1024

VT_ROWS = V_D + V7X_BF16_SUBLANES

QK_PAD = 2 * V7X_LANES
ROPE_HALF = ROPE_D // 2


def _vmem_limit(*byte_counts):
    return int(min(sum(byte_counts), V7X_VMEM_LIMIT_CAP))


def _nbytes(shape, dtype):
    return math.prod(shape) * jnp.dtype(dtype).itemsize


def _rmsnorm_proj_kernel(x_ref, g_ref, w_ref, n_ref, o_ref):
    x = x_ref[...]
    ms = jnp.mean(x * x, axis=-1, keepdims=True)
    n = (x * lax.rsqrt(ms + EPS) * g_ref[...]).astype(n_ref.dtype)
    n_ref[...] = n
    o_ref[...] = jnp.dot(n, w_ref[...], preferred_element_type=F32)


def _rmsnorm_proj(x, gain, w, *, tm):
    t, d = x.shape
    n = w.shape[1]
    return pl.pallas_call(
        _rmsnorm_proj_kernel,
        grid=(t // tm,),
        in_specs=[pl.BlockSpec((tm, d), lambda i: (i, 0)),
                  pl.BlockSpec((1, d), lambda i: (0, 0)),
                  pl.BlockSpec((d, n), lambda i: (0, 0))],
        out_specs=[pl.BlockSpec((tm, d), lambda i: (i, 0)),
                   pl.BlockSpec((tm, n), lambda i: (i, 0))],
        out_shape=[jax.ShapeDtypeStruct((t, d), BF16), jax.ShapeDtypeStruct((t, n), F32)],
        compiler_params=pltpu.CompilerParams(
            dimension_semantics=("arbitrary",),
            vmem_limit_bytes=_vmem_limit(4 * _nbytes((tm, d), F32), 2 * _nbytes((tm, d), BF16),
                                         2 * _nbytes((d, n), BF16), 3 * _nbytes((tm, n), F32),
                                         4 * 1024 * 1024)),
        name="norm_in_proj_mla",
    )(x, gain.reshape(1, d).astype(F32), w)


def _matmul_kernel(*refs, n_pairs, has_res, has_gain, ssq_dim, act, nk, rhs_transposed):
    refs = list(refs)
    lhs = [refs.pop(0) for _ in range(n_pairs)]
    rhs = [refs.pop(0) for _ in range(n_pairs)]
    res_ref = refs.pop(0) if has_res else None
    gain_ref = refs.pop(0) if has_gain else None
    ssq_in_ref = refs.pop(0) if ssq_dim else None
    o_ref = refs.pop(0)
    scaled_ref, ssq_out_ref = (refs.pop(0), refs.pop(0)) if has_gain else (None, None)
    acc_ref = refs.pop(0) if refs else None
    dims = (((1,), (1 if rhs_transposed else 0,)), ((), ()))

    def product():
        part = lax.dot_general(lhs[0][...], rhs[0][...], dims, preferred_element_type=F32)
        for a, b in zip(lhs[1:], rhs[1:]):
            part = part + lax.dot_general(a[...], b[...], dims, preferred_element_type=F32)
        return part

    def epilogue(v):
        if ssq_dim:
            v = v * lax.rsqrt(ssq_in_ref[:, :1] * (1.0 / ssq_dim) + EPS)
        if act == "relu2":
            v = jnp.square(jnp.maximum(v, 0.0))
        if has_res:
            v = v + res_ref[...]
        o_ref[...] = v.astype(o_ref.dtype)
        if has_gain:
            scaled_ref[...] = (v * gain_ref[...]).astype(scaled_ref.dtype)
            row_ss = jnp.broadcast_to(jnp.sum(v * v, axis=-1, keepdims=True), ssq_out_ref.shape)
            j = pl.program_id(1)

            @pl.when(j == 0)
            def _():
                ssq_out_ref[...] = row_ss

            @pl.when(j > 0)
            def _():
                ssq_out_ref[...] += row_ss

    if nk == 1:
        epilogue(product())
        return

    k = pl.program_id(2)

    if acc_ref is None:
        @pl.when(k == 0)
        def _():
            epilogue(product())

        @pl.when(k > 0)
        def _():
            o_ref[...] += product()

        return

    @pl.when(k == 0)
    def _():
        acc_ref[...] = product()

    @pl.when(jnp.logical_and(k > 0, k < nk - 1))
    def _():
        acc_ref[...] += product()

    @pl.when(k == nk - 1)
    def _():
        epilogue(acc_ref[...] + product())


def _matmul(lhs_list, rhs_list, *, residual=None, act=None, next_norm_gain=None, lhs_row_ssq=None,
            rhs_row_blocks=None, rhs_transposed=False, out_dtype, tm, tn, tk=None, name):
    m, kdim = lhs_list[0].shape
    n = rhs_list[0].shape[0 if rhs_transposed else 1]
    tk = kdim if tk is None else tk
    nk = kdim // tk
    assert m % tm == 0 and n % tn == 0 and kdim % tk == 0
    n_pairs = len(lhs_list)
    offsets = [0] * n_pairs if rhs_row_blocks is None else rhs_row_blocks
    if rhs_transposed:
        assert rhs_row_blocks is None
        rhs_specs = [pl.BlockSpec((tn, tk), lambda i, j, k: (j, k))] * n_pairs
    else:
        rhs_specs = [pl.BlockSpec((tk, tn), functools.partial(lambda i, j, k, off: (k + off, j), off=off))
                     for off in offsets]
    in_specs = [pl.BlockSpec((tm, tk), lambda i, j, k: (i, k))] * n_pairs + rhs_specs
    args = list(lhs_list) + list(rhs_list)
    vmem = [2 * n_pairs * (_nbytes((tm, tk), BF16) + _nbytes((tk, tn), BF16)),
            2 * _nbytes((tm, tn), out_dtype),
            2 * _nbytes((tm, tn), F32)]
    tile_spec = pl.BlockSpec((tm, tn), lambda i, j, k: (i, j))
    row_spec = pl.BlockSpec((tm, V7X_LANES), lambda i, j, k: (i, 0))
    out_specs = [tile_spec]
    out_shape = [jax.ShapeDtypeStruct((m, n), out_dtype)]
    if residual is not None:
        in_specs.append(tile_spec)
        args.append(residual)
        vmem.append(2 * _nbytes((tm, tn), residual.dtype))
    if next_norm_gain is not None:
        assert nk == 1
        in_specs.append(pl.BlockSpec((1, tn), lambda i, j, k: (0, j)))
        args.append(next_norm_gain.reshape(1, n).astype(F32))
        out_specs += [tile_spec, row_spec]
        out_shape += [jax.ShapeDtypeStruct((m, n), BF16), jax.ShapeDtypeStruct((m, V7X_LANES), F32)]
        vmem.append(2 * _nbytes((tm, tn), BF16) + 2 * _nbytes((tm, tn), F32)
                    + 2 * _nbytes((tm, V7X_LANES), F32))
    if lhs_row_ssq is not None:
        in_specs.append(row_spec)
        args.append(lhs_row_ssq)
        vmem.append(2 * _nbytes((tm, V7X_LANES), F32))
    scratch = []
    plain_f32_out = (out_dtype == F32 and act is None and next_norm_gain is None
                     and lhs_row_ssq is None)
    if nk > 1 and not plain_f32_out:
        scratch.append(pltpu.VMEM((tm, tn), F32))
        vmem.append(_nbytes((tm, tn), F32))
    outs = pl.pallas_call(
        functools.partial(_matmul_kernel, n_pairs=n_pairs, has_res=residual is not None,
                          has_gain=next_norm_gain is not None,
                          ssq_dim=kdim if lhs_row_ssq is not None else 0, act=act, nk=nk,
                          rhs_transposed=rhs_transposed),
        grid=(m // tm, n // tn, nk),
        in_specs=in_specs,
        out_specs=out_specs,
        out_shape=out_shape,
        scratch_shapes=scratch,
        compiler_params=pltpu.CompilerParams(
            dimension_semantics=("arbitrary", "arbitrary", "arbitrary"),
            vmem_limit_bytes=_vmem_limit(*vmem)),
        name=name,
    )(*args)
    return outs if next_norm_gain is not None else outs[0]


def _hgrn2_kernel(q_ref, f_ref, i_ref, g_ref, lb_ref, nw_ref, o_ref, state_ref, *, n_chunks, n_heads):
    c = HG_CHUNK

    @pl.when(pl.program_id(2) == 0)
    def _():
        state_ref[...] = jnp.zeros_like(state_ref)

    lb_raw = lb_ref[...]
    e = jnp.exp(lb_raw - jnp.max(lb_raw, axis=0, keepdims=True))
    lb_all = e[0:1, :] / jnp.sum(e, axis=0, keepdims=True)
    nw = nw_ref[...]

    row = lax.broadcasted_iota(jnp.int32, (c, c), 0)
    col = lax.broadcasted_iota(jnp.int32, (c, c), 1)
    causal = row >= col
    tri = jnp.where(causal, 1.0, 0.0).astype(BF16)

    nt_dims = (((1,), (1,)), ((), ()))
    tn_dims = (((0,), (0,)), ((), ()))
    head_cols = [slice(hd * HG_DK, (hd + 1) * HG_DK) for hd in range(n_heads)]

    def chunk(ci, carry):
        sl = pl.ds(pl.multiple_of(ci * c, c), c)
        q_raw = q_ref[sl, :]
        f_raw = f_ref[sl, :]
        q_half = 0.5 * q_raw
        q = q_half + q_half * jnp.tanh(q_half)
        f = lb_all + (1.0 - lb_all) * (0.5 + 0.5 * jnp.tanh(0.5 * f_raw))
        k = 1.0 - f
        logf = jnp.log(f)
        hi = logf.astype(BF16)
        lo = (logf - hi.astype(F32)).astype(BF16)
        b = (jnp.dot(tri, hi, preferred_element_type=F32)
             + jnp.dot(tri, lo, preferred_element_type=F32))
        b_last = b[c - 1:c, :]
        q_dec = (q * jnp.exp(b)).astype(BF16)
        decay = jnp.exp(b_last)
        k_inv_f32 = k * jnp.exp(-b)
        k_inv = k_inv_f32.astype(BF16)
        k_tail = (k_inv_f32 * decay).astype(BF16)
        vb = i_ref[sl, :].astype(BF16)

        a = [lax.dot_general(q_dec[:, s], k_inv[:, s], nt_dims, preferred_element_type=F32)
             for s in head_cols]
        a = [jnp.where(causal, x, 0.0).astype(BF16) for x in a]
        states = [state_ref[hd] for hd in range(n_heads)]
        o = [jnp.dot(a[hd], vb[:, s], preferred_element_type=F32)
             + lax.dot_general(q_dec[:, s], states[hd].astype(BF16), nt_dims, preferred_element_type=F32)
             for hd, s in enumerate(head_cols)]
        u = [lax.dot_general(vb[:, s], k_tail[:, s], tn_dims, preferred_element_type=F32)
             for s in head_cols]
        for hd, s in enumerate(head_cols):
            state_ref[hd] = states[hd] * decay[:, s] + u[hd]

        y = [x * lax.rsqrt(jnp.mean(x * x, axis=-1, keepdims=True) + EPS) * nw for x in o]
        g_raw = g_ref[sl, :]
        g_half = 0.5 * g_raw
        gate = g_half + g_half * jnp.tanh(g_half)
        o_ref[sl, :] = (jnp.concatenate(y, axis=1) * gate).astype(o_ref.dtype)
        return carry

    lax.fori_loop(0, n_chunks, chunk, 0, unroll=4)


def _hgrn2(proj_hg, hgrn_lb, out_norm_w, *, batch, seq, ct, heads_per_step):
    t = batch * seq
    hg = heads_per_step
    n_groups = HG_HEADS // hg
    nt = seq // ct
    assert seq % ct == 0 and ct % HG_CHUNK == 0 and HG_HEADS % hg == 0

    def col_spec(section):
        return pl.BlockSpec((ct, hg * HG_DK), lambda b, gg, tt: (b * nt + tt, section * n_groups + gg))

    blk = _nbytes((ct, hg * HG_DK), F32)
    return pl.pallas_call(
        functools.partial(_hgrn2_kernel, n_chunks=ct // HG_CHUNK, n_heads=hg),
        grid=(batch, n_groups, nt),
        in_specs=[col_spec(0), col_spec(1), col_spec(2), col_spec(3),
                  pl.BlockSpec((hgrn_lb.shape[0], hg * HG_DK), lambda b, gg, tt: (0, gg)),
                  pl.BlockSpec((1, HG_DV), lambda b, gg, tt: (0, 0))],
        out_specs=pl.BlockSpec((ct, hg * HG_DV), lambda b, gg, tt: (b * nt + tt, gg)),
        out_shape=jax.ShapeDtypeStruct((t, HG_HEADS * HG_DV), BF16),
        scratch_shapes=[pltpu.VMEM((hg, HG_DV, HG_DK), F32)],
        compiler_params=pltpu.CompilerParams(
            dimension_semantics=("arbitrary", "arbitrary", "arbitrary"),
            vmem_limit_bytes=_vmem_limit(8 * blk, 2 * _nbytes((ct, hg * HG_DV), BF16),
                                         8 * 1024 * 1024)),
        name="hgrn2",
    )(proj_hg, proj_hg, proj_hg, proj_hg, hgrn_lb.astype(F32), out_norm_w.reshape(1, HG_DV).astype(F32))


def _mla_prep_kernel(p_ref, pos_ref, qnw_ref, wuq_ref, kvnw_ref, wukv_ref, qhw_ref, khw_ref,
                     invf_ref, sgn_ref, qt_out, k_out, vt_out, *, scale):
    p = p_ref[...]
    c_q = p[:, :Q_RANK]
    c_kv = p[:, Q_RANK:Q_RANK + KV_RANK]
    k_pe = p[:, Q_RANK + KV_RANK:]

    def rms(x, w):
        return x * lax.rsqrt(jnp.mean(x * x, axis=-1, keepdims=True) + EPS) * w

    q_all = jnp.dot(rms(c_q, qnw_ref[...]).astype(BF16), wuq_ref[...], preferred_element_type=F32)
    kv_all = jnp.dot(rms(c_kv, kvnw_ref[...]).astype(BF16), wukv_ref[...], preferred_element_type=F32)

    ang = pos_ref[...].astype(F32) * invf_ref[...]
    cos = jnp.cos(ang)
    sin = jnp.sin(ang) * sgn_ref[...]

    def rope(x):
        return x * cos + pltpu.roll(x, V7X_LANES // 2, 1) * sin

    qhw = qhw_ref[...]
    khw = khw_ref[...]
    inv_d = 1.0 / QK_D
    k_pe_ss = jnp.sum(k_pe * k_pe, axis=-1, keepdims=True)
    k_pe_rot = rope(k_pe * khw[:, V7X_LANES:])

    for h in range(MLA_HEADS):
        qa = q_all[:, h * QK_PAD:h * QK_PAD + V7X_LANES]
        qb = q_all[:, h * QK_PAD + V7X_LANES:(h + 1) * QK_PAD]
        ss = jnp.sum(qa * qa, axis=-1, keepdims=True) + jnp.sum(qb * qb, axis=-1, keepdims=True)
        r = lax.rsqrt(ss * inv_d + EPS) * scale
        qt_out[0, h, :V7X_LANES, :] = (qa * r * qhw[:, :V7X_LANES]).T.astype(qt_out.dtype)
        qt_out[0, h, V7X_LANES:, :] = rope(qb * r * qhw[:, V7X_LANES:]).T.astype(qt_out.dtype)

        ka = kv_all[:, h * 2 * V7X_LANES:h * 2 * V7X_LANES + NOPE_D]
        vv = kv_all[:, h * 2 * V7X_LANES + NOPE_D:(h + 1) * 2 * V7X_LANES]
        ssk = jnp.sum(ka * ka, axis=-1, keepdims=True) + k_pe_ss
        rk = lax.rsqrt(ssk * inv_d + EPS)
        k_out[0, h, :, :V7X_LANES] = (ka * rk * khw[:, :V7X_LANES]).astype(k_out.dtype)
        k_out[0, h, :, V7X_LANES:] = (k_pe_rot * rk).astype(k_out.dtype)
        vt_out[0, h, 0, :V_D, :] = vv.T.astype(vt_out.dtype)
        vt_out[0, h, 0, V_D:, :] = jnp.ones((VT_ROWS - V_D, vv.shape[0]), vt_out.dtype)


def _pad_rope_cols(w):
    z = jnp.zeros(w.shape[:-1] + (ROPE_HALF,), w.dtype)
    return jnp.concatenate([w[..., :ROPE_HALF], z, w[..., ROPE_HALF:], z], axis=-1)


def _pad_head_cols(w):
    return jnp.concatenate([w[..., :NOPE_D], _pad_rope_cols(w[..., NOPE_D:])], axis=-1)


def _mla_prep(proj_mla, positions, q_norm_w, w_uq, kv_norm_w, w_ukv, q_head_norm_w, k_head_norm_w,
              *, batch, seq, tm):
    t = batch * seq
    nh = MLA_HEADS
    pw = proj_mla.shape[1]
    wuq_p = _pad_head_cols(w_uq.reshape(Q_RANK, nh, QK_D)).reshape(Q_RANK, nh * QK_PAD).astype(BF16)
    wukv = w_ukv.astype(BF16)
    qhw = _pad_head_cols(q_head_norm_w.reshape(1, QK_D)).astype(F32)
    khw = _pad_head_cols(k_head_norm_w.reshape(1, QK_D)).astype(F32)
    inv_freq = ROPE_THETA ** (-jnp.arange(0, ROPE_D, 2, dtype=F32) / ROPE_D)
    invf = _pad_rope_cols(jnp.concatenate([inv_freq, inv_freq]).reshape(1, ROPE_D))
    sgn = _pad_rope_cols(jnp.concatenate([-jnp.ones((ROPE_HALF,), F32),
                                          jnp.ones((ROPE_HALF,), F32)]).reshape(1, ROPE_D))
    nst = seq // tm
    const = lambda i: (0, 0)
    vmem = [2 * _nbytes((tm, pw), F32), 2 * _nbytes(wuq_p.shape, BF16), 2 * _nbytes(wukv.shape, BF16),
            2 * 2 * _nbytes((nh, tm, QK_PAD), BF16), 2 * _nbytes((nh, tm, VT_ROWS), BF16),
            4 * _nbytes((tm, nh * QK_PAD), F32), 4 * 1024 * 1024]
    scale = math.log2(math.e) / math.sqrt(QK_D)
    return pl.pallas_call(
        functools.partial(_mla_prep_kernel, scale=scale),
        grid=(t // tm,),
        in_specs=[pl.BlockSpec((tm, pw), lambda i: (i, 0)),
                  pl.BlockSpec((tm, 1), lambda i: (i, 0)),
                  pl.BlockSpec((1, Q_RANK), const),
                  pl.BlockSpec(wuq_p.shape, const),
                  pl.BlockSpec((1, KV_RANK), const),
                  pl.BlockSpec(wukv.shape, const),
                  pl.BlockSpec((1, QK_PAD), const),
                  pl.BlockSpec((1, QK_PAD), const),
                  pl.BlockSpec((1, V7X_LANES), const),
                  pl.BlockSpec((1, V7X_LANES), const)],
        out_specs=[pl.BlockSpec((1, nh, QK_PAD, tm), lambda i: (i // nst, 0, 0, i % nst)),
                   pl.BlockSpec((1, nh, tm, QK_PAD), lambda i: (i // nst, 0, i % nst, 0)),
                   pl.BlockSpec((1, nh, 1, VT_ROWS, tm), lambda i: (i // nst, 0, i % nst, 0, 0))],
        out_shape=[jax.ShapeDtypeStruct((batch, nh, QK_PAD, seq), BF16),
                   jax.ShapeDtypeStruct((batch, nh, seq, QK_PAD), BF16),
                   jax.ShapeDtypeStruct((batch, nh, nst, VT_ROWS, tm), BF16)],
        compiler_params=pltpu.CompilerParams(
            dimension_semantics=("arbitrary",),
            vmem_limit_bytes=_vmem_limit(*vmem)),
        name="mla_prep",
    )(proj_mla, positions.reshape(t, 1), q_norm_w.reshape(1, Q_RANK).astype(F32), wuq_p,
      kv_norm_w.reshape(1, KV_RANK).astype(F32), wukv, qhw, khw, invf, sgn)


def _flash_kernel(qt_ref, k_ref, vt_ref, *rest, tq, tk, n_cast):
    cast_in, rest = rest[:n_cast], rest[n_cast:]
    o_ref, cast_out, rest = rest[0], rest[1:1 + n_cast], rest[1 + n_cast:]
    acc_ref, st_a, st_b, p_a, p_b, al_a, al_b, cm_a, cm_b = rest

    qi = pl.program_id(2)
    n_pairs = (tq // tk // 2) * qi
    n_full = 2 * n_pairs
    vt_tile = vt_ref.shape[-1]
    n_vt = tk // vt_tile

    def scores(blk, st_ref, cm_ref, cols, masked):
        kj = k_ref[0, 0, pl.ds(pl.multiple_of(blk * tk, tk), tk), :]
        st = jnp.dot(kj, qt_ref[0, 0, :, cols], preferred_element_type=F32)
        if masked:
            st = causal_mask(blk, st, cols)
        st_ref[:, cols] = st
        cm_ref[:, cols] = jnp.max(st, axis=0, keepdims=True)

    def causal_mask(blk, st, cols):
        n_cols = cols.stop - cols.start
        k_idx = blk * tk + lax.broadcasted_iota(jnp.int32, (tk, n_cols), 0)
        q_idx = qi * tq + cols.start + lax.broadcasted_iota(jnp.int32, (tk, n_cols), 1)
        return jnp.where(k_idx <= q_idx, st, -jnp.inf)

    def remask(blk, st_ref, cm_ref, cols):
        st = causal_mask(blk, st_ref[:, cols], cols)
        st_ref[:, cols] = st
        cm_ref[:, cols] = jnp.max(st, axis=0, keepdims=True)

    def softmax(st_ref, cm_ref, p_ref, al_ref, m_prev, cols):
        m_new = jnp.maximum(m_prev, cm_ref[:, cols])
        p_ref[:, cols] = jnp.exp2(st_ref[:, cols] - m_new).astype(BF16)
        al_ref[:, cols] = jnp.exp2(m_prev - m_new)
        return m_new

    def accumulate(blk, p_ref, al_ref, cols):
        vt = jnp.concatenate([vt_ref[0, 0, blk * n_vt + t] for t in range(n_vt)], axis=1)
        acc_ref[:, cols] = al_ref[:, cols] * acc_ref[:, cols] + jnp.dot(
            vt, p_ref[:, cols], preferred_element_type=F32)

    n_chunk = tq // (2 * V7X_LANES)
    chunks = [slice(c * 2 * V7X_LANES, (c + 1) * 2 * V7X_LANES) for c in range(n_chunk)]
    for cols in chunks:
        scores(0, st_a, cm_a, cols, False)

    for src, dst in zip(cast_in, cast_out):
        dst[...] = src[...].astype(dst.dtype)

    acc_ref[...] = jnp.zeros_like(acc_ref)
    p_b[...] = jnp.zeros_like(p_b)
    al_b[...] = jnp.ones_like(al_b)
    m_run = jnp.full((1, tq), -jnp.inf, F32)

    def pipelined(blk_next, st_next, cm_next, st_cur, cm_cur, p_cur, al_cur, blk_prev, p_prev, al_prev, m):
        m_parts = []
        for cols in chunks:
            scores(blk_next, st_next, cm_next, cols, False)
            accumulate(blk_prev, p_prev, al_prev, cols)
            m_parts.append(softmax(st_cur, cm_cur, p_cur, al_cur, m[:, cols], cols))
        return jnp.concatenate(m_parts, axis=1)

    def body(i, m):
        h = 2 * i
        m = pipelined(h + 1, st_b, cm_b, st_a, cm_a, p_a, al_a, jnp.maximum(h - 1, 0), p_b, al_b, m)
        return pipelined(h + 2, st_a, cm_a, st_b, cm_b, p_b, al_b, h, p_a, al_a, m)

    m_run = lax.fori_loop(0, n_pairs, body, m_run)
    dv = o_ref.shape[-1]
    sets = ((st_a, cm_a, p_a, al_a), (st_b, cm_b, p_b, al_b))
    for cols in chunks:
        own_block = cols.start // tk
        if own_block >= 1:
            scores(n_full + 1, st_b, cm_b, cols, own_block == 1)
        else:
            remask(n_full, st_a, cm_a, cols)
        accumulate(jnp.maximum(n_full - 1, 0), p_b, al_b, cols)
        m_cols = m_run[:, cols]
        for d in range(own_block + 1):
            st_x, cm_x, p_x, al_x = sets[d % 2]
            if d >= 2:
                scores(n_full + d, st_x, cm_x, cols, d == own_block)
            m_cols = softmax(st_x, cm_x, p_x, al_x, m_cols, cols)
            accumulate(n_full + d, p_x, al_x, cols)
        o_ref[0, cols, :] = (acc_ref[:dv, cols] / acc_ref[dv:dv + 1, cols]).T.astype(o_ref.dtype)


def _flash_attention(qt, k, vt, cast_weights, *, tq, tk):
    b, h, dqk, s = qt.shape
    nvb, vt_rows, tv = vt.shape[2:]
    dv = V_D
    nq = s // tq
    n_steps = b * h * nq
    assert tk % tv == 0 and s % tq == 0 and tq % (2 * tk) == 0 and vt_rows == VT_ROWS
    vmem = [2 * _nbytes((dqk, tq), BF16), 2 * _nbytes((s, dqk), BF16), 2 * _nbytes((nvb, vt_rows, tv), BF16),
            2 * _nbytes((tq, dv), BF16), 3 * _nbytes((vt_rows, tq), F32),
            2 * _nbytes((tk, tq), F32), 2 * _nbytes((tk, tq), BF16),
            3 * _nbytes((tk, tq), F32), 4 * 1024 * 1024]
    step_slab = lambda bb, hh, qi: ((bb * h + hh) * nq + qi, 0)
    cast_specs, cast_shapes = [], []
    for w, n_rows in cast_weights:
        rows, n_cols = n_rows // n_steps, w.shape[1]
        assert n_rows % n_steps == 0 and rows % V7X_BF16_SUBLANES == 0 and n_cols % V7X_LANES == 0
        cast_specs.append(pl.BlockSpec((rows, n_cols), step_slab))
        cast_shapes.append(jax.ShapeDtypeStruct((n_rows, n_cols), BF16))
        vmem.append(3 * _nbytes((rows, n_cols), F32) + 2 * _nbytes((rows, n_cols), BF16))
    outs = pl.pallas_call(
        functools.partial(_flash_kernel, tq=tq, tk=tk, n_cast=len(cast_weights)),
        grid=(b, h, nq),
        in_specs=[pl.BlockSpec((1, 1, dqk, tq), lambda bb, hh, qi: (bb, hh, 0, qi)),
                  pl.BlockSpec((1, 1, s, dqk), lambda bb, hh, qi: (bb, hh, 0, 0)),
                  pl.BlockSpec((1, 1, nvb, vt_rows, tv), lambda bb, hh, qi: (bb, hh, 0, 0, 0))] + cast_specs,
        out_specs=[pl.BlockSpec((1, tq, dv), lambda bb, hh, qi: (bb, qi, hh))] + cast_specs,
        out_shape=[jax.ShapeDtypeStruct((b, s, h * dv), BF16)] + cast_shapes,
        scratch_shapes=[pltpu.VMEM((vt_rows, tq), F32),
                        pltpu.VMEM((tk, tq), F32), pltpu.VMEM((tk, tq), F32),
                        pltpu.VMEM((tk, tq), BF16), pltpu.VMEM((tk, tq), BF16),
                        pltpu.VMEM((1, tq), F32), pltpu.VMEM((1, tq), F32),
                        pltpu.VMEM((1, tq), F32), pltpu.VMEM((1, tq), F32)],
        compiler_params=pltpu.CompilerParams(
            dimension_semantics=("arbitrary", "arbitrary", "arbitrary"),
            vmem_limit_bytes=_vmem_limit(*vmem)),
        name="mla_flash",
    )(qt, k, vt, *[w for w, _ in cast_weights])
    return outs[0], outs[1:]


def kernel(x, positions, norm1_w, w_in, hgrn_lb, hgrn_out_norm_w, mla_q_norm_w, w_uq, mla_kv_norm_w,
           w_ukv, q_head_norm_w, k_head_norm_w, w_o, norm2_w, w_up, w_down):
    batch, seq, d = x.shape
    t = batch * seq
    depth = norm1_w.shape[0]
    hg_w = 2 * HG_HEADS * HG_DK + 2 * HG_HEADS * HG_DV
    h = x.reshape(t, d)
    for l in range(depth):
        w_in_t = jnp.swapaxes(w_in[l], 0, 1)
        w_mla_t = w_in_t[hg_w:]
        w_mla = jnp.swapaxes(jnp.concatenate(
            [w_mla_t[:Q_RANK + KV_RANK],
             jnp.swapaxes(_pad_rope_cols(jnp.swapaxes(w_mla_t[Q_RANK + KV_RANK:], 0, 1)), 0, 1)],
            axis=0), 0, 1).astype(BF16)
        n1, proj_mla = _rmsnorm_proj(h, norm1_w[l], w_mla, tm=256)
        qt, k, vt = _mla_prep(proj_mla, positions, mla_q_norm_w[l], w_uq[l], mla_kv_norm_w[l], w_ukv[l],
                              q_head_norm_w[l], k_head_norm_w[l], batch=batch, seq=seq, tm=256)
        o_mla, (w_hg_t, w_o_l, w_up_l, w_down_l) = _flash_attention(
            qt, k, vt, [(w_in_t, hg_w), (w_o[l], d), (w_up[l], d), (w_down[l], w_down.shape[1])],
            tq=1024, tk=512)
        o_mla = o_mla.reshape(t, MLA_HEADS * V_D)

        proj_hg = _matmul([n1], [w_hg_t], rhs_transposed=True, out_dtype=F32, tm=1024, tn=1024,
                          name="in_proj_hg")
        assert depth == 1
        o_hg = _hgrn2(proj_hg, hgrn_lb, hgrn_out_norm_w[l], batch=batch, seq=seq, ct=512,
                      heads_per_step=8)

        h, h_gained, h_ssq = _matmul([o_hg, o_mla], [w_o_l, w_o_l], rhs_row_blocks=[0, 1], residual=h,
                                     next_norm_gain=norm2_w[l], out_dtype=F32, tm=1024, tn=512,
                                     name="out_proj")
        hid = _matmul([h_gained], [w_up_l], lhs_row_ssq=h_ssq, act="relu2", out_dtype=BF16,
                      tm=1024, tn=1024, name="mlp_up")
        h = _matmul([hid], [w_down_l], residual=h, out_dtype=F32, tm=1024, tn=1024,
                    tk=4096, name="mlp_down")
    return h.reshape(batch, seq, d)
```
